```python
import math
import jax, jax.numpy as jnp
from jax import lax
import numpy as np

D_MODEL = 1024
BATCH = 16
SEQ = 4096
DEPTH = 4

GRID_W = 64
CTX_LEN = 256
N_MIXERS = 2
N_ATTN_LAYERS = (DEPTH + 1) // 2
N_HGRN_LAYERS = DEPTH // 2
EPS = 1e-6

HEAD_DIM = 64
N_HEADS = D_MODEL // HEAD_DIM
N_KV_HEADS = 4
GROUP = N_HEADS // N_KV_HEADS
WINDOW = 128
BLOCK = 128
ROPE_THETA = 10000.0
ATTN_IN_W = (N_HEADS + 2 * N_KV_HEADS) * HEAD_DIM

HG_EXPAND = 128
HG_HEADS = D_MODEL // HG_EXPAND
HG_DK = HG_EXPAND
HG_DV = D_MODEL // HG_HEADS
HG_DF = HG_HEADS * HG_DK
HG_CHUNK = 32
HG_SPLITS = [HG_DF, 2 * HG_DF, 3 * HG_DF, 3 * HG_DF + D_MODEL]
HG_IN_W = 3 * HG_DF + 2 * D_MODEL

D_FF = 2816
CONV_W = 3

kernel_name = 'hybrid_swa_hgrn2_convglu_dit'


def rms_norm(x, gain):
    xf = x.astype(jnp.float32)
    y = xf * lax.rsqrt(jnp.mean(xf * xf, axis=-1, keepdims=True) + EPS)
    return (y * gain.astype(jnp.float32)).astype(x.dtype)


def axial_angles(rows):
    row = jnp.repeat(jnp.arange(rows, dtype=jnp.float32), GRID_W)
    col = jnp.tile(jnp.arange(GRID_W, dtype=jnp.float32), rows)
    n_pairs = HEAD_DIM // 4
    inv = ROPE_THETA ** (-jnp.arange(n_pairs, dtype=jnp.float32) / n_pairs)
    return row[:, None] * inv, col[:, None] * inv


def rope_rotate(x, ang):
    x1, x2 = jnp.split(x, 2, axis=-1)
    cos = jnp.cos(ang)[None, :, None, :].astype(x.dtype)
    sin = jnp.sin(ang)[None, :, None, :].astype(x.dtype)
    return jnp.concatenate([x1 * cos - x2 * sin, x1 * sin + x2 * cos], axis=-1)


def apply_axial_rope(x, ang_r, ang_c):
    xr, xc = jnp.split(x, 2, axis=-1)
    return jnp.concatenate([rope_rotate(xr, ang_r), rope_rotate(xc, ang_c)], axis=-1)


def sink_softmax(scores, sink_g):
    s_sink = jnp.broadcast_to(sink_g[None, :, :, None, None].astype(jnp.float32), scores.shape[:-1] + (1,))
    p = jax.nn.softmax(jnp.concatenate([scores, s_sink], axis=-1), axis=-1)
    return p[..., :-1]


def attn_project(h, w_in, q_gain, k_gain):
    b, t, _ = h.shape
    q, k, v = jnp.split(h @ w_in, [N_HEADS * HEAD_DIM, (N_HEADS + N_KV_HEADS) * HEAD_DIM], axis=-1)
    q = rms_norm(q.reshape(b, t, N_HEADS, HEAD_DIM), q_gain)
    k = rms_norm(k.reshape(b, t, N_KV_HEADS, HEAD_DIM), k_gain)
    v = v.reshape(b, t, N_KV_HEADS, HEAD_DIM)
    return q, k, v


def windowed_gqa(hx, hc, w_in, w_out, q_gain, k_gain, sink, ang_r, ang_c, need_ctx_out):
    B, L, _ = hx.shape
    nb = L // BLOCK
    scale = HEAD_DIM ** -0.5
    qx, kx, vx = attn_project(hx, w_in, q_gain, k_gain)
    qc, kc, vc = attn_project(hc, w_in, q_gain, k_gain)
    qx = apply_axial_rope(qx, ang_r, ang_c)
    kx = apply_axial_rope(kx, ang_r, ang_c)
    kc_t = kc.transpose(0, 2, 1, 3)
    vc_t = vc.transpose(0, 2, 1, 3)
    pad = ((0, 0), (0, 0), (BLOCK, BLOCK), (0, 0))
    kx_p = jnp.pad(kx.transpose(0, 2, 1, 3), pad)
    vx_p = jnp.pad(vx.transpose(0, 2, 1, 3), pad)
    q_blocks = qx.reshape(B, nb, BLOCK, N_KV_HEADS, GROUP, HEAD_DIM).transpose(1, 0, 3, 4, 2, 5)
    sink_g = sink.reshape(N_KV_HEADS, GROUP)
    q_off = jnp.arange(BLOCK, dtype=jnp.int32)
    k_off = jnp.arange(3 * BLOCK, dtype=jnp.int32) - BLOCK

    def block(args):
        n, qb = args
        start = n * BLOCK
        kb = lax.dynamic_slice_in_dim(kx_p, start, 3 * BLOCK, axis=2)
        vb = lax.dynamic_slice_in_dim(vx_p, start, 3 * BLOCK, axis=2)
        qi = start + q_off
        kj = start + k_off
        valid = ((kj >= 0) & (kj < L))[None, :] & (jnp.abs(qi[:, None] - kj[None, :]) <= WINDOW)
        s_win = jnp.einsum('bkgqd,bkjd->bkgqj', qb, kb).astype(jnp.float32) * scale
        s_win = jnp.where(valid, s_win, -jnp.inf)
        s_ctx = jnp.einsum('bkgqd,bkcd->bkgqc', qb, kc_t).astype(jnp.float32) * scale
        p = sink_softmax(jnp.concatenate([s_win, s_ctx], axis=-1), sink_g).astype(vb.dtype)
        return (jnp.einsum('bkgqj,bkjd->bkgqd', p[..., :3 * BLOCK], vb)
                + jnp.einsum('bkgqc,bkcd->bkgqd', p[..., 3 * BLOCK:], vc_t))

    o = lax.map(block, (jnp.arange(nb, dtype=jnp.int32), q_blocks))
    ox = o.transpose(1, 0, 4, 2, 3, 5).reshape(B, L, N_HEADS * HEAD_DIM)
    yc = None
    if need_ctx_out:
        C = hc.shape[1]
        qcg = qc.reshape(B, C, N_KV_HEADS, GROUP, HEAD_DIM).transpose(0, 2, 3, 1, 4)
        s = jnp.einsum('bkgqd,bkcd->bkgqc', qcg, kc_t).astype(jnp.float32) * scale
        p = sink_softmax(s, sink_g).astype(vc.dtype)
        oc = jnp.einsum('bkgqc,bkcd->bkgqd', p, vc_t).transpose(0, 3, 1, 2, 4).reshape(B, C, N_HEADS * HEAD_DIM)
        yc = oc @ w_out
    return ox @ w_out, yc


def chunk_scan(q, k, g, v, s0, need_out):
    B, T, H, dk = q.shape
    dv = v.shape[-1]
    n = T // HG_CHUNK

    def to_chunks(a):
        return a.reshape(B, n, HG_CHUNK, H, a.shape[-1]).transpose(1, 0, 3, 2, 4).astype(jnp.float32)

    tri = jnp.tril(jnp.ones((HG_CHUNK, HG_CHUNK), dtype=bool))

    def step(S, inp):
        qc, kc, gc, vc = inp
        b = jnp.cumsum(gc, axis=2)
        b_last = b[:, :, -1:, :]
        k_dec = kc * jnp.exp(b_last - b)
        S_new = jnp.exp(b_last[:, :, 0, :])[..., None] * S + jnp.einsum('bhck,bhcv->bhkv', k_dec, vc)
        if not need_out:
            return S_new, None
        o_inter = jnp.einsum('bhck,bhkv->bhcv', qc * jnp.exp(b), S)
        diff = jnp.where(tri[:, :, None], b[:, :, :, None, :] - b[:, :, None, :, :], -jnp.inf)
        a = jnp.einsum('bhtk,bhsk,bhtsk->bhts', qc, kc, jnp.exp(diff))
        return S_new, o_inter + jnp.einsum('bhts,bhsv->bhtv', a, vc)

    S_fin, o = lax.scan(step, s0, (to_chunks(q), to_chunks(k), to_chunks(g), to_chunks(v)))
    if need_out:
        o = o.transpose(1, 0, 3, 2, 4).reshape(B, T, H, dv)
    return o, S_fin


def hgrn2_bidir(hx, hc, w_in, w_out, o_gain, lb, need_ctx_out):
    def project(h):
        b, t, _ = h.shape
        q, f_fw, f_bw, inp, gate = jnp.split(h @ w_in, HG_SPLITS, axis=-1)
        q = jax.nn.silu(q).reshape(b, t, HG_HEADS, HG_DK)
        inp = inp.reshape(b, t, HG_HEADS, HG_DV)

        def forget(fl):
            f = lb + (1.0 - lb) * jax.nn.sigmoid(fl.astype(jnp.float32))
            return (1.0 - f).reshape(b, t, HG_HEADS, HG_DK), jnp.log(f).reshape(b, t, HG_HEADS, HG_DK)

        return q, forget(f_fw), forget(f_bw), inp, gate

    flip = lambda a: jnp.flip(a, axis=1)
    qx, (kxf, gxf), (kxb, gxb), vx, gx = project(hx)
    qc, (kcf, gcf), (kcb, gcb), vc, gc = project(hc)
    s0 = jnp.zeros((hc.shape[0], HG_HEADS, HG_DK, HG_DV), jnp.float32)
    oc_f, sc_f = chunk_scan(qc, kcf, gcf, vc, s0, need_ctx_out)
    oc_b, sc_b = chunk_scan(flip(qc), flip(kcb), flip(gcb), flip(vc), s0, need_ctx_out)
    ox_f, _ = chunk_scan(qx, kxf, gxf, vx, sc_f, True)
    ox_b, _ = chunk_scan(flip(qx), flip(kxb), flip(gxb), flip(vx), sc_b, True)

    def readout(o, gate, h):
        o = rms_norm(o, o_gain).astype(h.dtype) * jax.nn.silu(gate).reshape(o.shape)
        return o.reshape(h.shape[0], h.shape[1], D_MODEL) @ w_out

    yx = readout(ox_f + flip(ox_b), gx, hx)
    yc = readout(oc_f + flip(oc_b), gc, hc) if need_ctx_out else None
    return yx, yc


def dwconv3(u, w, b):
    up = jnp.pad(u, ((0, 0), (1, 1), (0, 0)))
    return up[:, :-2] * w[0] + up[:, 1:-1] * w[1] + up[:, 2:] * w[2] + b


def conv_glu(h, w_up, conv_w, conv_b, w_down):
    gate, val = jnp.split(h @ w_up, 2, axis=-1)
    return (jax.nn.silu(dwconv3(gate, conv_w, conv_b)) * val) @ w_down


def setup_inputs(seed: int = 0) -> dict:
    key = jax.random.key(seed)
    ks = jax.random.split(key, 21)
    D = D_MODEL

    def nrm(k, shape, scale):
        return jax.random.normal(k, shape, jnp.float32) * scale

    def gain(k, shape):
        return 1.0 + 0.05 * jax.random.normal(k, shape, jnp.float32)

    return {
        'x': nrm(ks[0], (BATCH, SEQ, D), 1.0),
        'c': nrm(ks[1], (BATCH, D), 1.0),
        'ctx': nrm(ks[2], (BATCH, CTX_LEN, D), 1.0),
        'c_ctx': nrm(ks[3], (D,), 1.0),
        'ada_w': nrm(ks[4], (DEPTH, D, 6 * D), 0.5 * D ** -0.5),
        'ada_b': nrm(ks[5], (DEPTH, 6 * D), 0.02),
        'norm1_g': gain(ks[6], (DEPTH, D)),
        'norm2_g': gain(ks[7], (DEPTH, D)),
        'attn_w_in': nrm(ks[8], (N_ATTN_LAYERS, D, ATTN_IN_W), D ** -0.5),
        'attn_w_out': nrm(ks[9], (N_ATTN_LAYERS, N_HEADS * HEAD_DIM, D), (N_HEADS * HEAD_DIM) ** -0.5),
        'attn_q_gain': gain(ks[10], (N_ATTN_LAYERS, HEAD_DIM)),
        'attn_k_gain': gain(ks[11], (N_ATTN_LAYERS, HEAD_DIM)),
        'attn_sink': nrm(ks[12], (N_ATTN_LAYERS, N_HEADS), 0.5),
        'hgrn_w_in': nrm(ks[13], (N_HGRN_LAYERS, D, HG_IN_W), D ** -0.5),
        'hgrn_w_out': nrm(ks[14], (N_HGRN_LAYERS, D, D), D ** -0.5),
        'hgrn_o_gain': gain(ks[15], (N_HGRN_LAYERS, HG_DV)),
        'hgrn_lb_logits': nrm(ks[16], (DEPTH, HG_DF), 0.5),
        'ffn_w_up': nrm(ks[17], (DEPTH, D, 2 * D_FF), D ** -0.5),
        'ffn_conv_w': nrm(ks[18], (DEPTH, CONV_W, D_FF), CONV_W ** -0.5),
        'ffn_conv_b': nrm(ks[19], (DEPTH, D_FF), 0.02),
        'ffn_w_down': nrm(ks[20], (DEPTH, D_FF, D), D_FF ** -0.5),
    }


def reference(x, c, ctx, c_ctx, ada_w, ada_b, norm1_g, norm2_g, attn_w_in, attn_w_out, attn_q_gain,
              attn_k_gain, attn_sink, hgrn_w_in, hgrn_w_out, hgrn_o_gain, hgrn_lb_logits, ffn_w_up,
              ffn_conv_w, ffn_conv_b, ffn_w_down):
    L = x.shape[1]
    rows = L // GRID_W
    ang_r, ang_c = axial_angles(rows)
    lb_prob = jax.nn.softmax(hgrn_lb_logits.astype(jnp.float32), axis=0)
    lb_sched = jnp.cumsum(lb_prob, axis=0) - lb_prob[0]
    for layer in range(DEPTH):
        last = layer == DEPTH - 1
        j = layer // N_MIXERS
        mod = jax.nn.silu(c) @ ada_w[layer] + ada_b[layer]
        mod_c = jax.nn.silu(c_ctx) @ ada_w[layer] + ada_b[layer]
        sh1, sc1, g1, sh2, sc2, g2 = jnp.split(mod[:, None, :], 6, axis=-1)
        csh1, csc1, cg1, csh2, csc2, cg2 = jnp.split(mod_c, 6)
        hx = rms_norm(x, norm1_g[layer]) * (1.0 + sc1) + sh1
        hc = rms_norm(ctx, norm1_g[layer]) * (1.0 + csc1) + csh1
        if layer % N_MIXERS == 0:
            yx, yc = windowed_gqa(hx, hc, attn_w_in[j], attn_w_out[j], attn_q_gain[j], attn_k_gain[j],
                                  attn_sink[j], ang_r, ang_c, not last)
        else:
            yx, yc = hgrn2_bidir(hx, hc, hgrn_w_in[j], hgrn_w_out[j], hgrn_o_gain[j], lb_sched[layer], not last)
        x = x + g1 * yx
        hx2 = rms_norm(x, norm2_g[layer]) * (1.0 + sc2) + sh2
        x = x + g2 * conv_glu(hx2, ffn_w_up[layer], ffn_conv_w[layer], ffn_conv_b[layer], ffn_w_down[layer])
        if not last:
            ctx = ctx + cg1 * yc
            hc2 = rms_norm(ctx, norm2_g[layer]) * (1.0 + csc2) + csh2
            ctx = ctx + cg2 * conv_glu(hc2, ffn_w_up[layer], ffn_conv_w[layer], ffn_conv_b[layer], ffn_w_down[layer])
    return x
```

```python
import functools
import math

import jax
import jax.numpy as jnp
import numpy as np
from jax import lax
from jax.experimental import pallas as pl
from jax.experimental.pallas import tpu as pltpu

F32 = jnp.float32
BF16 = jnp.bfloat16

EPS = 1e-6
HEAD_DIM = 64
N_HEADS = 16
N_KV_HEADS = 4
GROUP = N_HEADS // N_KV_HEADS
WINDOW = 128
ROPE_THETA = 10000.0
GRID_W = 64
HG_HEADS = 8
HG_DK = 128
CHUNK = 128
N_LEVELS = 7
TILE = 256
QBLK = 128
LANES = 128
SUBLANES = 8
NEG_BIG = -1e30
VMEM_LIMIT = 56 * 1024 * 1024


def _silu(x):
    return x / (1.0 + jnp.exp(-x))


def _sigmoid(x):
    return 1.0 / (1.0 + jnp.exp(-x))


def _dot(a, b):
    return jnp.dot(a, b, preferred_element_type=F32)


def _dot_nt(a, b):
    return lax.dot_general(a, b, (((1,), (1,)), ((), ())), preferred_element_type=F32)


def _dot_tn(a, b):
    return lax.dot_general(a, b, (((0,), (0,)), ((), ())), preferred_element_type=F32)


def _norm_mod(x, gain, scale, shift):
    ms = jnp.mean(x * x, axis=-1, keepdims=True)
    return (x * lax.rsqrt(ms + EPS)) * gain * (1.0 + scale) + shift


def _params(sem):
    return pltpu.CompilerParams(dimension_semantics=sem, vmem_limit_bytes=VMEM_LIMIT)


def _resident(shape):
    nd = len(shape)
    return pl.BlockSpec(shape, lambda *_: (0,) * nd, pipeline_mode=pl.Buffered(1))


def _mod_kernel(cc_ref, w_ref, b_ref, o_ref):
    a = _silu(cc_ref[...]).astype(BF16)
    o_ref[...] = _dot(a, w_ref[...].astype(BF16)) + b_ref[...]


def _modulation(cc, ada_w, ada_b):
    depth, d, n = ada_w.shape
    r = cc.shape[0]
    tn = 1536
    return pl.pallas_call(
        _mod_kernel,
        grid=(depth, n // tn),
        in_specs=[
            pl.BlockSpec((r, d), lambda l, j: (0, 0)),
            pl.BlockSpec((None, d, tn), lambda l, j: (l, 0, j)),
            pl.BlockSpec((None, 1, tn), lambda l, j: (l, 0, j)),
        ],
        out_specs=pl.BlockSpec((None, r, tn), lambda l, j: (l, 0, j)),
        out_shape=jax.ShapeDtypeStruct((depth, r, n), F32),
        compiler_params=_params(("parallel", "parallel")),
        name="modulation",
    )(cc, ada_w, ada_b.reshape(depth, 1, n))


class _Stream:
    def __init__(self, batch, n_lat, n_ctx, d):
        self.batch, self.n_lat, self.n_ctx, self.d = batch, n_lat, n_ctx, d
        self.grid = (batch, n_lat + n_ctx)

    def tile(self, width, col=0):
        return pl.BlockSpec((None, TILE, width), lambda b, t: (b, t, col))

    def mod(self):
        n_lat, batch = self.n_lat, self.batch
        return pl.BlockSpec((None, 1, 6 * self.d), lambda b, t: (jnp.where(t >= n_lat, batch, b), 0, 0))

    def row(self, width):
        return pl.BlockSpec((1, width), lambda b, t: (0, 0))


def _attn_proj_kernel(x_ref, mod_ref, gain_ref, w_ref, qg_ref, kg_ref, cos_ref, sin_ref, gm_ref, rot_ref,
                      q_ref, k_ref, v_ref, *, d):
    mod = mod_ref[...]
    h = _norm_mod(x_ref[...], gain_ref[...], mod[:, d:2 * d], mod[:, 0:d]).astype(BF16)
    acc = _dot(h, w_ref[...])
    cos, sin = cos_ref[...], sin_ref[...]
    gm, rot = gm_ref[...], rot_ref[...]
    blk = 4 * HEAD_DIM
    n_qblk = N_HEADS * HEAD_DIM // blk
    for c in range(n_qblk + 1):
        xc = acc[:, c * blk:(c + 1) * blk]
        ms = _dot((xc * xc).astype(BF16), gm)
        gain = qg_ref[...] if c < n_qblk else kg_ref[...]
        xn = xc * lax.rsqrt(ms + EPS) * gain
        y = xn * cos + _dot(xn.astype(BF16), rot) * sin
        if c < n_qblk:
            q_ref[:, c * blk:(c + 1) * blk] = (y * (HEAD_DIM ** -0.5)).astype(BF16)
        else:
            k_ref[...] = y.astype(BF16)
    v_ref[...] = acc[:, (n_qblk + 1) * blk:].astype(BF16)


def _attn_proj(st, xs, mod, gain, w, qg, kg, cos, sin, gm, rot):
    b, t, d = xs.shape
    nq, nk = N_HEADS * HEAD_DIM, N_KV_HEADS * HEAD_DIM
    return pl.pallas_call(
        functools.partial(_attn_proj_kernel, d=d),
        grid=st.grid,
        in_specs=[st.tile(d), st.mod(), st.row(d), _resident(w.shape), st.row(nk), st.row(nk),
                  pl.BlockSpec((TILE, nk), lambda b, t: (t, 0)), pl.BlockSpec((TILE, nk), lambda b, t: (t, 0)),
                  _resident(gm.shape), _resident(rot.shape)],
        out_specs=[st.tile(nq), st.tile(nk), st.tile(nk)],
        out_shape=[jax.ShapeDtypeStruct((b, t, nq), BF16), jax.ShapeDtypeStruct((b, t, nk), BF16),
                   jax.ShapeDtypeStruct((b, t, nk), BF16)],
        compiler_params=_params(("parallel", "parallel")),
        name="attn_proj",
    )(xs, mod, gain, w, qg, kg, cos, sin, gm, rot)


def _attn_kernel(sink_ref, q_ref, k_ref, v_ref, o_ref, *, seq, ctx):
    n = pl.program_id(1)
    n_lat = seq // QBLK
    wlen = QBLK + 2 * WINDOW
    is_ctx = n >= n_lat
    ws = pl.multiple_of(jnp.clip(n * QBLK - WINDOW, 0, seq - wlen), QBLK)
    kall = jnp.concatenate([k_ref[pl.ds(ws, wlen), :], k_ref[seq:seq + ctx, :]], axis=0)
    vall = jnp.concatenate([v_ref[pl.ds(ws, wlen), :], v_ref[seq:seq + ctx, :]], axis=0)
    nkeys = wlen + ctx
    qpos = n * QBLK + lax.broadcasted_iota(jnp.int32, (QBLK, wlen), 0)
    kpos = ws + lax.broadcasted_iota(jnp.int32, (QBLK, wlen), 1)
    in_window = (jnp.abs(qpos - kpos) <= WINDOW) & jnp.logical_not(is_ctx)
    bias = jnp.where(in_window, 0.0, NEG_BIG).astype(F32)
    left = lax.broadcasted_iota(jnp.int32, (1, LANES), 1) < HEAD_DIM
    zero = jnp.zeros((), BF16)
    for p in range(N_KV_HEADS // 2):
        kb = kall[:, p * LANES:(p + 1) * LANES]
        vb = vall[:, p * LANES:(p + 1) * LANES]
        kbd = jnp.concatenate([jnp.where(left, kb, zero), jnp.where(left, zero, kb)], axis=0)
        vbd = jnp.concatenate([jnp.where(left, vb, zero), jnp.where(left, zero, vb)], axis=0)
        for g in range(GROUP):
            blk = p * GROUP + g
            s = _dot_nt(q_ref[:, blk * LANES:(blk + 1) * LANES], kbd)
            probs, invs = [], []
            for half in range(2):
                sink = sink_ref[2 * blk + half]
                sw = s[:, half * nkeys:half * nkeys + wlen] + bias
                sc = s[:, half * nkeys + wlen:(half + 1) * nkeys]
                m = jnp.maximum(jnp.maximum(jnp.max(sw, axis=-1, keepdims=True),
                                            jnp.max(sc, axis=-1, keepdims=True)), sink)
                pw, pc = jnp.exp(sw - m), jnp.exp(sc - m)
                den = jnp.sum(pw, axis=-1, keepdims=True) + jnp.sum(pc, axis=-1, keepdims=True) + jnp.exp(sink - m)
                probs += [pw.astype(BF16), pc.astype(BF16)]
                invs.append(1.0 / den)
            o = _dot(jnp.concatenate(probs, axis=1), vbd)
            o_ref[:, blk * LANES:(blk + 1) * LANES] = (o * jnp.where(left, invs[0], invs[1])).astype(BF16)


def _attention(sink, q, k, v, seq, ctx):
    b, t, nq = q.shape
    nk = k.shape[-1]
    return pl.pallas_call(
        functools.partial(_attn_kernel, seq=seq, ctx=ctx),
        grid=(b, t // QBLK),
        in_specs=[pl.BlockSpec(memory_space=pltpu.SMEM),
                  pl.BlockSpec((None, QBLK, nq), lambda b, n: (b, n, 0)),
                  pl.BlockSpec((None, t, nk), lambda b, n: (b, 0, 0)),
                  pl.BlockSpec((None, t, nk), lambda b, n: (b, 0, 0))],
        out_specs=pl.BlockSpec((None, QBLK, nq), lambda b, n: (b, n, 0)),
        out_shape=jax.ShapeDtypeStruct((b, t, nq), BF16),
        compiler_params=_params(("parallel", "parallel")),
        name="attention",
    )(sink, q, k, v)


def _chunk_cumsum(g, reverse):
    row = lax.broadcasted_iota(jnp.int32, (CHUNK, 1), 0)
    d = 1
    while d < CHUNK:
        if reverse:
            g = g + jnp.where(row < CHUNK - d, pltpu.roll(g, CHUNK - d, axis=0), 0.0)
        else:
            g = g + jnp.where(row >= d, pltpu.roll(g, d, axis=0), 0.0)
        d *= 2
    return g


def _hgrn_proj_kernel(x_ref, mod_ref, gain_ref, w_ref, lbl_ref, q_ref, kf_ref, bf_ref, kb_ref, bb_ref, v_ref,
                      gate_ref, *, d, layer):
    mod = mod_ref[...]
    h = _norm_mod(x_ref[...], gain_ref[...], mod[:, d:2 * d], mod[:, 0:d]).astype(BF16)
    lg = lbl_ref[...]
    e = jnp.exp(lg - jnp.max(lg, axis=0, keepdims=True))
    lb = jnp.sum(e[1:layer + 1], axis=0, keepdims=True) / jnp.sum(e, axis=0, keepdims=True)
    w = HG_HEADS * HG_DK
    q_ref[...] = _silu(_dot(h, w_ref[:, 0:w])).astype(BF16)
    for seg, (k_ref, b_ref) in enumerate(((kf_ref, bf_ref), (kb_ref, bb_ref))):
        f = lb + (1.0 - lb) * _sigmoid(_dot(h, w_ref[:, (1 + seg) * w:(2 + seg) * w]))
        k_ref[...] = (1.0 - f).astype(BF16)
        g = jnp.log(f)
        for c in range(TILE // CHUNK):
            b_ref[c * CHUNK:(c + 1) * CHUNK, :] = _chunk_cumsum(g[c * CHUNK:(c + 1) * CHUNK], reverse=seg == 1)
    v_ref[...] = _dot(h, w_ref[:, 3 * w:3 * w + d]).astype(BF16)
    gate_ref[...] = _silu(_dot(h, w_ref[:, 3 * w + d:3 * w + 2 * d])).astype(BF16)


def _hgrn_proj(st, xs, mod, gain, w, lb_logits, layer):
    b, t, d = xs.shape
    wd = HG_HEADS * HG_DK
    shapes = [(wd, BF16), (wd, BF16), (wd, F32), (wd, BF16), (wd, F32), (d, BF16), (d, BF16)]
    return pl.pallas_call(
        functools.partial(_hgrn_proj_kernel, d=d, layer=layer),
        grid=st.grid,
        in_specs=[st.tile(d), st.mod(), st.row(d), _resident(w.shape), _resident(lb_logits.shape)],
        out_specs=[st.tile(n) for n, _ in shapes],
        out_shape=[jax.ShapeDtypeStruct((b, t, n), dt) for n, dt in shapes],
        compiler_params=_params(("parallel", "parallel")),
        name="hgrn_proj",
    )(xs, mod, gain, w, lb_logits)


def _level_table(reverse):
    t = np.arange(CHUNK)[:, None]
    s = np.arange(CHUNK)[None, :]
    x = t ^ s
    lvl = np.where(x > 0, np.floor(np.log2(np.maximum(x, 1))).astype(np.int32), N_LEVELS)
    later = (t < s) if reverse else (t > s)
    lvl = np.where(later | (t == s), lvl, -1).astype(np.int32)
    return np.concatenate([lvl, lvl], axis=1)


def _mid_rows(b2, half, reverse):
    n = b2.shape[-1]
    blk = 2 * half
    m = half if reverse else half - 1
    if blk >= SUBLANES:
        b3 = b2.reshape(CHUNK // blk, blk, n)
        return jnp.broadcast_to(b3[:, m:m + 1, :], b3.shape).reshape(CHUNK, n)
    b3 = b2.reshape(CHUNK // SUBLANES, SUBLANES, n)
    r = lax.broadcasted_iota(jnp.int32, b3.shape, 1)
    mid = jnp.broadcast_to(b3[:, m:m + 1, :], b3.shape)
    for j in range(1, SUBLANES // blk):
        mid = jnp.where(r >= j * blk, jnp.broadcast_to(b3[:, j * blk + m:j * blk + m + 1, :], b3.shape), mid)
    return mid.reshape(CHUNK, n)


def _scan_kernel(lvl_ref, q_ref, k_ref, b_ref, v_ref, o_ref, st_ref, *, reverse):
    @pl.when(pl.program_id(1) == 0)
    def _():
        st_ref[...] = jnp.zeros(st_ref.shape, F32)

    pw = 2 * HG_DK
    lvl = lvl_ref[...]
    lane = lax.broadcasted_iota(jnp.int32, (1, pw), 1)
    left = lane < HG_DK
    same_head = (lax.broadcasted_iota(jnp.int32, (pw, pw), 0) < HG_DK) == (lane < HG_DK)
    zero = jnp.zeros((), BF16)
    edge = 0 if reverse else CHUNK - 1

    def pair_rows(x):
        return jnp.concatenate([jnp.where(left, x, zero), jnp.where(left, zero, x)], axis=0)

    for p in range(HG_HEADS // 2):
        cs = slice(p * pw, (p + 1) * pw)
        q2 = q_ref[:, cs].astype(F32)
        k2 = k_ref[:, cs].astype(F32)
        b2 = b_ref[:, cs]
        v2 = v_ref[:, cs]
        b_edge = b2[edge:edge + 1, :]
        st = st_ref[p]
        o = _dot_nt((q2 * jnp.exp(b2)).astype(BF16), st.astype(BF16))
        qk = q2 * k2
        diag = jnp.where(lane < CHUNK, jnp.sum(qk[:, :HG_DK], axis=-1, keepdims=True),
                         jnp.sum(qk[:, HG_DK:], axis=-1, keepdims=True))
        a = jnp.where(lvl == N_LEVELS, diag, 0.0)
        for l in range(N_LEVELS):
            e = jnp.exp(-jnp.abs(b2 - _mid_rows(b2, 2 ** l, reverse)))
            a_l = _dot_nt((q2 * e).astype(BF16), pair_rows((k2 * e).astype(BF16)))
            a = jnp.where(lvl == l, a_l, a)
        o = o + _dot(a.astype(BF16), pair_rows(v2))
        o_ref[:, cs] = o.astype(BF16)
        kdec = (k2 * jnp.exp(b_edge - b2)).astype(BF16)
        upd = _dot_tn(v2, kdec)
        st_ref[p] = st * jnp.exp(b_edge) + jnp.where(same_head, upd, 0.0)


def _hgrn_scan(lvl, q, k, bcum, v, seq, ctx, reverse):
    b, t, w = q.shape
    n_lat, n_ctx = seq // CHUNK, ctx // CHUNK
    n = n_lat + n_ctx

    def chunk(i):
        if reverse:
            return jnp.where(i < n_ctx, n - 1 - i, n - 1 - i)
        return jnp.where(i < n_ctx, n_lat + i, i - n_ctx)

    spec = pl.BlockSpec((None, CHUNK, w), lambda b, i: (b, chunk(i), 0))
    return pl.pallas_call(
        functools.partial(_scan_kernel, reverse=reverse),
        grid=(b, n),
        in_specs=[_resident(lvl.shape), spec, spec, spec, spec],
        out_specs=spec,
        out_shape=jax.ShapeDtypeStruct((b, t, w), BF16),
        scratch_shapes=[pltpu.VMEM((HG_HEADS // 2, 2 * HG_DK, 2 * HG_DK), F32)],
        compiler_params=_params(("parallel", "arbitrary")),
        name="hgrn_scan_bwd" if reverse else "hgrn_scan_fwd",
    )(lvl, q, k, bcum, v)


def _attn_out_kernel(x_ref, mod_ref, o_ref, w_ref, y_ref, *, d):
    gate = mod_ref[...][:, 2 * d:3 * d]
    y_ref[...] = x_ref[...] + gate * _dot(o_ref[...], w_ref[...])


def _attn_out(st, xs, mod, o, w):
    b, t, d = xs.shape
    return pl.pallas_call(
        functools.partial(_attn_out_kernel, d=d),
        grid=st.grid,
        in_specs=[st.tile(d), st.mod(), st.tile(o.shape[-1]), _resident(w.shape)],
        out_specs=st.tile(d),
        out_shape=jax.ShapeDtypeStruct((b, t, d), F32),
        compiler_params=_params(("parallel", "parallel")),
        name="attn_out",
    )(xs, mod, o, w)


def _hgrn_out_kernel(x_ref, mod_ref, of_ref, ob_ref, gate_ref, og_ref, w_ref, y_ref, *, d):
    o = of_ref[...].astype(F32) + ob_ref[...].astype(F32)
    og = og_ref[...]
    parts = []
    for hd in range(HG_HEADS):
        oh = o[:, hd * HG_DK:(hd + 1) * HG_DK]
        ms = jnp.mean(oh * oh, axis=-1, keepdims=True)
        parts.append(oh * lax.rsqrt(ms + EPS) * og)
    r = jnp.concatenate(parts, axis=1) * gate_ref[...].astype(F32)
    gate = mod_ref[...][:, 2 * d:3 * d]
    y_ref[...] = x_ref[...] + gate * _dot(r.astype(BF16), w_ref[...])


def _hgrn_out(st, xs, mod, o_f, o_b, gate, o_gain, w):
    b, t, d = xs.shape
    return pl.pallas_call(
        functools.partial(_hgrn_out_kernel, d=d),
        grid=st.grid,
        in_specs=[st.tile(d), st.mod(), st.tile(d), st.tile(d), st.tile(d), st.row(HG_DK), _resident(w.shape)],
        out_specs=st.tile(d),
        out_shape=jax.ShapeDtypeStruct((b, t, d), F32),
        compiler_params=_params(("parallel", "parallel")),
        name="hgrn_out",
    )(xs, mod, o_f, o_b, gate, o_gain, w)


FFN_CHUNK = 512


def _ffn_up_kernel(x_ref, mod_ref, gain_ref, w_ref, o_ref, *, d):
    mod = mod_ref[...]
    h = _norm_mod(x_ref[...], gain_ref[...], mod[:, 4 * d:5 * d], mod[:, 3 * d:4 * d]).astype(BF16)
    for c in range(w_ref.shape[1] // FFN_CHUNK):
        cs = slice(c * FFN_CHUNK, (c + 1) * FFN_CHUNK)
        o_ref[:, cs] = _dot(h, w_ref[:, cs]).astype(BF16)


def _ffn_up(st, xs, mod, gain, w):
    b, t, d = xs.shape
    n = w.shape[1]
    return pl.pallas_call(
        functools.partial(_ffn_up_kernel, d=d),
        grid=st.grid,
        in_specs=[st.tile(d), st.mod(), st.row(d), _resident(w.shape)],
        out_specs=st.tile(n),
        out_shape=jax.ShapeDtypeStruct((b, t, n), BF16),
        compiler_params=_params(("parallel", "parallel")),
        name="ffn_up",
    )(xs, mod, gain, w)


CONV_COLS = 256
HALO = 16


def _ffn_down_kernel(x_ref, mod_ref, gate_ref, val_ref, prev_ref, next_ref, cw_ref, cb_ref, w_ref, y_ref, u_ref,
                     *, d, n_lat, n_ctx):
    t = pl.program_id(1)
    has_prev = jnp.logical_and(t != 0, t != n_lat).astype(F32)
    has_next = jnp.logical_and(t != n_lat - 1, t != n_lat + n_ctx - 1).astype(F32)
    row = lax.broadcasted_iota(jnp.int32, (TILE, 1), 0)
    for c in range(gate_ref.shape[1] // CONV_COLS):
        cs = slice(c * CONV_COLS, (c + 1) * CONV_COLS)
        g = gate_ref[:, cs].astype(F32)
        before = prev_ref[:, cs].astype(F32)[HALO - 1:HALO] * has_prev
        after = next_ref[:, cs].astype(F32)[0:1] * has_next
        up = jnp.where(row == 0, before, pltpu.roll(g, 1, axis=0))
        dn = jnp.where(row == TILE - 1, after, pltpu.roll(g, TILE - 1, axis=0))
        conv = up * cw_ref[0:1, cs] + g * cw_ref[1:2, cs] + dn * cw_ref[2:3, cs] + cb_ref[:, cs]
        u_ref[:, cs] = (_silu(conv) * val_ref[:, cs].astype(F32)).astype(BF16)
    gate = mod_ref[...][:, 5 * d:6 * d]
    y_ref[...] = x_ref[...] + gate * _dot(u_ref[...], w_ref[...])


def _ffn_down(st, xs, mod, gv, conv_w, conv_b, w):
    b, t, d = xs.shape
    f = w.shape[0]
    per_tile = TILE // HALO
    last = t // HALO - 1
    return pl.pallas_call(
        functools.partial(_ffn_down_kernel, d=d, n_lat=st.n_lat, n_ctx=st.n_ctx),
        grid=st.grid,
        in_specs=[st.tile(d), st.mod(), st.tile(f, 0), st.tile(f, 1),
                  pl.BlockSpec((None, HALO, f), lambda b, t: (b, jnp.maximum(t * per_tile - 1, 0), 0)),
                  pl.BlockSpec((None, HALO, f), lambda b, t: (b, jnp.minimum((t + 1) * per_tile, last), 0)),
                  _resident(conv_w.shape), _resident(conv_b.shape), _resident(w.shape)],
        out_specs=st.tile(d),
        out_shape=jax.ShapeDtypeStruct((b, t, d), F32),
        scratch_shapes=[pltpu.VMEM((TILE, f), BF16)],
        compiler_params=_params(("parallel", "parallel")),
        name="ffn_down",
    )(xs, mod, gv, gv, gv, gv, conv_w, conv_b, w)


def _rope_tables(seq, ctx):
    rows = seq // GRID_W
    row = jnp.repeat(jnp.arange(rows, dtype=F32), GRID_W)
    col = jnp.tile(jnp.arange(GRID_W, dtype=F32), rows)
    n_pairs = HEAD_DIM // 4
    inv = ROPE_THETA ** (-jnp.arange(n_pairs, dtype=F32) / n_pairs)
    ang_r, ang_c = row[:, None] * inv, col[:, None] * inv
    cos = jnp.concatenate([jnp.cos(ang_r)] * 2 + [jnp.cos(ang_c)] * 2, axis=1)
    sin = jnp.concatenate([-jnp.sin(ang_r), jnp.sin(ang_r), -jnp.sin(ang_c), jnp.sin(ang_c)], axis=1)
    cos = jnp.concatenate([cos, jnp.ones((ctx, HEAD_DIM), F32)], axis=0)
    sin = jnp.concatenate([sin, jnp.zeros((ctx, HEAD_DIM), F32)], axis=0)
    return jnp.tile(cos, (1, 4)), jnp.tile(sin, (1, 4))


def _head_block_constants():
    blk = 4 * HEAD_DIM
    i = np.arange(blk)
    gm = (i[:, None] // HEAD_DIM == i[None, :] // HEAD_DIM).astype(np.float32) / HEAD_DIM
    quarter = HEAD_DIM // 4
    partner = np.where((i % (2 * quarter)) < quarter, i + quarter, i - quarter)
    rot = (i[:, None] == partner[None, :]).astype(np.float32)
    return jnp.asarray(gm, BF16), jnp.asarray(rot, BF16)


def _query_head_order():
    return np.array([(2 * p + half) * GROUP + g for p in range(N_KV_HEADS // 2) for g in range(GROUP)
                     for half in range(2)])


def kernel(x, c, ctx, c_ctx, ada_w, ada_b, norm1_g, norm2_g, attn_w_in, attn_w_out, attn_q_gain, attn_k_gain,
           attn_sink, hgrn_w_in, hgrn_w_out, hgrn_o_gain, hgrn_lb_logits, ffn_w_up, ffn_conv_w, ffn_conv_b,
           ffn_w_down):
    batch, seq, d = x.shape
    n_ctx_tok = ctx.shape[1]
    depth = ada_w.shape[0]
    assert seq % TILE == 0 and n_ctx_tok % TILE == 0 and seq >= QBLK + 2 * WINDOW
    st = _Stream(batch, seq // TILE, n_ctx_tok // TILE, d)

    rows = -(-(batch + 1) // SUBLANES) * SUBLANES
    cc = jnp.zeros((rows, d), F32).at[:batch].set(c).at[batch].set(c_ctx)
    mod_all = _modulation(cc, ada_w, ada_b)[:, :batch + 1].reshape(depth, batch + 1, 1, 6 * d)

    cos, sin = _rope_tables(seq, n_ctx_tok)
    gm, rot = _head_block_constants()
    order = _query_head_order()
    nq = N_HEADS * HEAD_DIM
    lvl_f, lvl_b = jnp.asarray(_level_table(False)), jnp.asarray(_level_table(True))

    xs = jnp.concatenate([x, ctx], axis=1)
    for layer in range(depth):
        j = layer // 2
        mod = mod_all[layer]
        g1 = norm1_g[layer].reshape(1, d)
        if layer % 2 == 0:
            w_in = attn_w_in[j]
            wq = w_in[:, :nq].reshape(d, N_HEADS, HEAD_DIM)[:, order].reshape(d, nq)
            w_in = jnp.concatenate([wq, w_in[:, nq:]], axis=1).astype(BF16)
            w_out = attn_w_out[j].reshape(N_HEADS, HEAD_DIM, d)[order].reshape(nq, d).astype(BF16)
            qg = jnp.tile(attn_q_gain[j], 4).reshape(1, 4 * HEAD_DIM)
            kg = jnp.tile(attn_k_gain[j], 4).reshape(1, 4 * HEAD_DIM)
            q, k, v = _attn_proj(st, xs, mod, g1, w_in, qg, kg, cos, sin, gm, rot)
            o = _attention(attn_sink[j][order], q, k, v, seq, n_ctx_tok)
            xs = _attn_out(st, xs, mod, o, w_out)
        else:
            q, kf, bf, kb, bb, v, gate = _hgrn_proj(st, xs, mod, g1, hgrn_w_in[j].astype(BF16), hgrn_lb_logits,
                                                    layer)
            o_f = _hgrn_scan(lvl_f, q, kf, bf, v, seq, n_ctx_tok, reverse=False)
            o_b = _hgrn_scan(lvl_b, q, kb, bb, v, seq, n_ctx_tok, reverse=True)
            xs = _hgrn_out(st, xs, mod, o_f, o_b, gate, hgrn_o_gain[j].reshape(1, HG_DK),
                           hgrn_w_out[j].astype(BF16))
        gv = _ffn_up(st, xs, mod, norm2_g[layer].reshape(1, d), ffn_w_up[layer].astype(BF16))
        xs = _ffn_down(st, xs, mod, gv, ffn_conv_w[layer], ffn_conv_b[layer].reshape(1, -1),
                       ffn_w_down[layer].astype(BF16))
    return xs[:, :seq]
```

```python
import functools
import math

import jax
import jax.numpy as jnp
import numpy as np
from jax import lax
from jax.experimental import pallas as pl
from jax.experimental.pallas import tpu as pltpu

F32 = jnp.float32
BF16 = jnp.bfloat16

EPS = 1e-6
HEAD_DIM = 64
N_HEADS = 16
N_KV_HEADS = 4
GROUP = N_HEADS // N_KV_HEADS
WINDOW = 128
ROPE_THETA = 10000.0
GRID_W = 64
HG_HEADS = 8
HG_DK = 128
CHUNK = 128
N_LEVELS = 7
TILE = 256
QBLK = 128
LANES = 128
SUBLANES = 8
NEG_BIG = -1e30
LOG2E = math.log2(math.e)
VMEM_LIMIT = 56 * 1024 * 1024


def _silu(x):
    return x / (1.0 + jnp.exp(-x))


def _sigmoid(x):
    return 1.0 / (1.0 + jnp.exp(-x))


def _dot(a, b):
    return jnp.dot(a, b, preferred_element_type=F32)


def _dot_nt(a, b):
    return lax.dot_general(a, b, (((1,), (1,)), ((), ())), preferred_element_type=F32)


def _dot_tn(a, b):
    return lax.dot_general(a, b, (((0,), (0,)), ((), ())), preferred_element_type=F32)


def _norm_mod(x, gain, scale, shift):
    ms = jnp.mean(x * x, axis=-1, keepdims=True)
    return (x * lax.rsqrt(ms + EPS)) * gain * (1.0 + scale) + shift


def _params(sem):
    return pltpu.CompilerParams(dimension_semantics=sem, vmem_limit_bytes=VMEM_LIMIT)


def _resident(shape):
    nd = len(shape)
    return pl.BlockSpec(shape, lambda *_: (0,) * nd, pipeline_mode=pl.Buffered(1))


def _mod_kernel(cc_ref, w_ref, b_ref, o_ref):
    a = _silu(cc_ref[...]).astype(BF16)
    o_ref[...] = _dot(a, w_ref[...].astype(BF16)) + b_ref[...]


def _modulation(cc, ada_w, ada_b):
    depth, d, n = ada_w.shape
    r = cc.shape[0]
    tn = 1536
    return pl.pallas_call(
        _mod_kernel,
        grid=(depth, n // tn),
        in_specs=[
            pl.BlockSpec((r, d), lambda l, j: (0, 0)),
            pl.BlockSpec((None, d, tn), lambda l, j: (l, 0, j)),
            pl.BlockSpec((None, 1, tn), lambda l, j: (l, 0, j)),
        ],
        out_specs=pl.BlockSpec((None, r, tn), lambda l, j: (l, 0, j)),
        out_shape=jax.ShapeDtypeStruct((depth, r, n), F32),
        compiler_params=_params(("parallel", "parallel")),
        name="modulation",
    )(cc, ada_w, ada_b.reshape(depth, 1, n))


class _Stream:
    def __init__(self, batch, n_lat, n_ctx, d):
        self.batch, self.n_lat, self.n_ctx, self.d = batch, n_lat, n_ctx, d
        self.grid = (batch, n_lat + n_ctx)
        self.rows = (n_lat + n_ctx) * TILE

    def tile(self, width, col=0):
        return pl.BlockSpec((None, TILE, width), lambda b, t: (b, t, col))

    def sources(self, lat, ctx):
        n_lat = self.n_lat
        off = n_lat if ctx is lat else 0
        return [pl.BlockSpec((None, TILE, self.d), lambda b, t: (b, jnp.minimum(t, n_lat - 1), 0)),
                pl.BlockSpec((None, TILE, self.d), lambda b, t: (b, off + jnp.maximum(t - n_lat, 0), 0))]

    def read(self, lat_ref, ctx_ref):
        return jnp.where(pl.program_id(1) >= self.n_lat, ctx_ref[...], lat_ref[...])

    def mod(self):
        n_lat, batch = self.n_lat, self.batch
        return pl.BlockSpec((None, 1, 6 * self.d), lambda b, t: (jnp.where(t >= n_lat, batch, b), 0, 0))

    def row(self, width):
        return pl.BlockSpec((1, width), lambda b, t: (0, 0))


def _attn_proj_kernel(x_ref, c_ref, mod_ref, gain_ref, w_ref, qg_ref, kg_ref, cos_ref, sin_ref, gm_ref, rot_ref,
                      q_ref, k_ref, v_ref, *, st):
    d = st.d
    mod = mod_ref[...]
    h = _norm_mod(st.read(x_ref, c_ref), gain_ref[...], mod[:, d:2 * d], mod[:, 0:d]).astype(BF16)
    acc = _dot(h, w_ref[...])
    cos, sin = cos_ref[...], sin_ref[...]
    gm, rot = gm_ref[...], rot_ref[...]
    blk = 4 * HEAD_DIM
    n_qblk = N_HEADS * HEAD_DIM // blk
    for c in range(n_qblk + 1):
        xc = acc[:, c * blk:(c + 1) * blk]
        ms = _dot((xc * xc).astype(BF16), gm)
        gain = qg_ref[...] if c < n_qblk else kg_ref[...]
        xn = xc * lax.rsqrt(ms + EPS) * gain
        y = xn * cos + _dot(xn.astype(BF16), rot) * sin
        if c < n_qblk:
            q_ref[:, c * blk:(c + 1) * blk] = (y * (HEAD_DIM ** -0.5 * LOG2E)).astype(BF16)
        else:
            k_ref[...] = y.astype(BF16)
    v_ref[...] = acc[:, (n_qblk + 1) * blk:].astype(BF16)


def _attn_proj(st, lat, ctx, mod, gain, w, qg, kg, cos, sin, gm, rot):
    b, d = st.batch, st.d
    t = st.rows
    nq, nk = N_HEADS * HEAD_DIM, N_KV_HEADS * HEAD_DIM
    return pl.pallas_call(
        functools.partial(_attn_proj_kernel, st=st),
        grid=st.grid,
        in_specs=[*st.sources(lat, ctx), st.mod(), st.row(d), _resident(w.shape), st.row(nk), st.row(nk),
                  pl.BlockSpec((TILE, nk), lambda b, t: (t, 0)), pl.BlockSpec((TILE, nk), lambda b, t: (t, 0)),
                  _resident(gm.shape), _resident(rot.shape)],
        out_specs=[st.tile(nq), st.tile(nk), st.tile(nk)],
        out_shape=[jax.ShapeDtypeStruct((b, t, nq), BF16), jax.ShapeDtypeStruct((b, t, nk), BF16),
                   jax.ShapeDtypeStruct((b, t, nk), BF16)],
        compiler_params=_params(("parallel", "parallel")),
        name="attn_proj",
    )(lat, ctx, mod, gain, w, qg, kg, cos, sin, gm, rot)


def _attn_kernel(sink_ref, q_ref, k_ref, v_ref, o_ref, *, seq, ctx):
    n = pl.program_id(1)
    n_lat = seq // QBLK
    wlen = QBLK + 2 * WINDOW
    is_ctx = n >= n_lat
    ws = pl.multiple_of(jnp.clip(n * QBLK - WINDOW, 0, seq - wlen), QBLK)
    kall = jnp.concatenate([k_ref[pl.ds(ws, wlen), :], k_ref[seq:seq + ctx, :]], axis=0)
    vall = jnp.concatenate([v_ref[pl.ds(ws, wlen), :], v_ref[seq:seq + ctx, :]], axis=0)
    nkeys = wlen + ctx
    qpos = n * QBLK + lax.broadcasted_iota(jnp.int32, (QBLK, wlen), 0)
    kpos = ws + lax.broadcasted_iota(jnp.int32, (QBLK, wlen), 1)
    in_window = (jnp.abs(qpos - kpos) <= WINDOW) & jnp.logical_not(is_ctx)
    bias = jnp.where(in_window, 0.0, NEG_BIG).astype(F32)
    left = lax.broadcasted_iota(jnp.int32, (1, LANES), 1) < HEAD_DIM
    kv_of_lane = lax.broadcasted_iota(jnp.int32, (1, N_KV_HEADS * HEAD_DIM), 1) // HEAD_DIM
    zero = jnp.zeros((), BF16)
    kbd = [jnp.concatenate([jnp.where(left, kall[:, p * LANES:(p + 1) * LANES], zero),
                            jnp.where(left, zero, kall[:, p * LANES:(p + 1) * LANES])], axis=0)
           for p in range(N_KV_HEADS // 2)]
    vbd = jnp.concatenate([jnp.where(kv_of_lane == kh, vall, zero) for kh in range(N_KV_HEADS)], axis=0)
    blk = N_KV_HEADS * HEAD_DIM
    for g in range(GROUP):
        probs, invs = [], []
        for p in range(N_KV_HEADS // 2):
            q2 = q_ref[:, g * blk + p * LANES:g * blk + (p + 1) * LANES]
            s = _dot_nt(q2, kbd[p])
            for half in range(2):
                sink = sink_ref[g * N_KV_HEADS + 2 * p + half] * LOG2E
                sw = s[:, half * nkeys:half * nkeys + wlen] + bias
                sc = s[:, half * nkeys + wlen:(half + 1) * nkeys]
                m = jnp.maximum(jnp.maximum(jnp.max(sw, axis=-1, keepdims=True),
                                            jnp.max(sc, axis=-1, keepdims=True)), sink)
                pw, pc = jnp.exp2(sw - m), jnp.exp2(sc - m)
                den = jnp.sum(pw, axis=-1, keepdims=True) + jnp.sum(pc, axis=-1, keepdims=True) + jnp.exp2(sink - m)
                probs += [pw.astype(BF16), pc.astype(BF16)]
                invs.append(1.0 / den)
        o = _dot(jnp.concatenate(probs, axis=1), vbd)
        inv = jnp.where(kv_of_lane == 0, invs[0], jnp.where(kv_of_lane == 1, invs[1],
                                                            jnp.where(kv_of_lane == 2, invs[2], invs[3])))
        o_ref[:, g * blk:(g + 1) * blk] = (o * inv).astype(BF16)


def _attention(sink, q, k, v, seq, ctx):
    b, t, nq = q.shape
    nk = k.shape[-1]
    return pl.pallas_call(
        functools.partial(_attn_kernel, seq=seq, ctx=ctx),
        grid=(b, t // QBLK),
        in_specs=[pl.BlockSpec(memory_space=pltpu.SMEM),
                  pl.BlockSpec((None, QBLK, nq), lambda b, n: (b, n, 0)),
                  pl.BlockSpec((None, t, nk), lambda b, n: (b, 0, 0)),
                  pl.BlockSpec((None, t, nk), lambda b, n: (b, 0, 0))],
        out_specs=pl.BlockSpec((None, QBLK, nq), lambda b, n: (b, n, 0)),
        out_shape=jax.ShapeDtypeStruct((b, t, nq), BF16),
        compiler_params=_params(("parallel", "parallel")),
        name="attention",
    )(sink, q, k, v)


PROJ_COLS = 256


def _chunk_cumsum(g, reverse):
    n = g.shape[-1]
    groups = CHUNK // SUBLANES
    g = g.reshape(groups, SUBLANES, n)
    r = lax.broadcasted_iota(jnp.int32, g.shape, 1)
    d = 1
    while d < SUBLANES:
        if reverse:
            g = g + jnp.where(r < SUBLANES - d, pltpu.roll(g, SUBLANES - d, axis=1), 0.0)
        else:
            g = g + jnp.where(r >= d, pltpu.roll(g, d, axis=1), 0.0)
        d *= 2
    g = g.reshape(CHUNK, n)
    edge = 0 if reverse else SUBLANES - 1
    parts = [None] * groups
    carry = None
    for j in (reversed(range(groups)) if reverse else range(groups)):
        blk = g[j * SUBLANES:(j + 1) * SUBLANES]
        if carry is not None:
            blk = blk + carry
        parts[j] = blk
        carry = blk[edge:edge + 1]
    return jnp.concatenate(parts, axis=0)


def _hgrn_proj_kernel(x_ref, mod_ref, gain_ref, w_ref, lbl_ref, q_ref, kf_ref, bf_ref, kb_ref, bb_ref, v_ref,
                      gate_ref, *, d, layer):
    mod = mod_ref[...]
    h = _norm_mod(x_ref[...], gain_ref[...], mod[:, d:2 * d], mod[:, 0:d]).astype(BF16)
    lg = lbl_ref[...]
    e = jnp.exp(lg - jnp.max(lg, axis=0, keepdims=True))
    lb = jnp.sum(e[1:layer + 1], axis=0, keepdims=True) / jnp.sum(e, axis=0, keepdims=True)
    w = HG_HEADS * HG_DK
    for c in range(w // PROJ_COLS):
        cs = slice(c * PROJ_COLS, (c + 1) * PROJ_COLS)
        q_ref[:, cs] = _silu(_dot(h, w_ref[:, cs])).astype(BF16)
        for seg, (k_ref, b_ref) in enumerate(((kf_ref, bf_ref), (kb_ref, bb_ref))):
            ws = slice((1 + seg) * w + c * PROJ_COLS, (1 + seg) * w + (c + 1) * PROJ_COLS)
            f = lb[:, cs] + (1.0 - lb[:, cs]) * _sigmoid(_dot(h, w_ref[:, ws]))
            k_ref[:, cs] = (1.0 - f).astype(BF16)
            g = jnp.log2(f)
            for r in range(TILE // CHUNK):
                rs = slice(r * CHUNK, (r + 1) * CHUNK)
                b_ref[rs, cs] = _chunk_cumsum(g[rs], reverse=seg == 1)
    for c in range(d // PROJ_COLS):
        cs = slice(c * PROJ_COLS, (c + 1) * PROJ_COLS)
        v_ref[:, cs] = _dot(h, w_ref[:, 3 * w + c * PROJ_COLS:3 * w + (c + 1) * PROJ_COLS]).astype(BF16)
        gs = slice(3 * w + d + c * PROJ_COLS, 3 * w + d + (c + 1) * PROJ_COLS)
        gate_ref[:, cs] = _silu(_dot(h, w_ref[:, gs])).astype(BF16)


def _hgrn_proj(st, xs, mod, gain, w, lb_logits, layer):
    b, t, d = xs.shape
    wd = HG_HEADS * HG_DK
    shapes = [(wd, BF16), (wd, BF16), (wd, F32), (wd, BF16), (wd, F32), (d, BF16), (d, BF16)]
    return pl.pallas_call(
        functools.partial(_hgrn_proj_kernel, d=d, layer=layer),
        grid=st.grid,
        in_specs=[st.tile(d), st.mod(), st.row(d), _resident(w.shape), _resident(lb_logits.shape)],
        out_specs=[st.tile(n) for n, _ in shapes],
        out_shape=[jax.ShapeDtypeStruct((b, t, n), dt) for n, dt in shapes],
        compiler_params=_params(("parallel", "parallel")),
        name="hgrn_proj",
    )(xs, mod, gain, w, lb_logits)


def _level_table(reverse):
    t = np.arange(CHUNK)[:, None]
    s = np.arange(CHUNK)[None, :]
    x = t ^ s
    lvl = np.where(x > 0, np.floor(np.log2(np.maximum(x, 1))).astype(np.int32), N_LEVELS)
    later = (t < s) if reverse else (t > s)
    lvl = np.where(later | (t == s), lvl, -1).astype(np.float32)
    return np.concatenate([lvl, lvl], axis=1)


def _neg_abs(x):
    sign = jnp.int32(-2 ** 31)
    return lax.bitcast_convert_type(lax.bitcast_convert_type(x, jnp.int32) | sign, F32)


def _mid_rows(b2, half, reverse):
    n = b2.shape[-1]
    blk = 2 * half
    m = half if reverse else half - 1
    if half == 1:
        b3 = b2.reshape(CHUNK // SUBLANES, SUBLANES, n)
        odd = lax.broadcasted_iota(jnp.int32, b3.shape, 1) % 2 == 1
        if reverse:
            return jnp.where(odd, b3, pltpu.roll(b3, SUBLANES - 1, axis=1)).reshape(CHUNK, n)
        return jnp.where(odd, pltpu.roll(b3, 1, axis=1), b3).reshape(CHUNK, n)
    if blk >= SUBLANES:
        b3 = b2.reshape(CHUNK // blk, blk, n)
        return jnp.broadcast_to(b3[:, m:m + 1, :], b3.shape).reshape(CHUNK, n)
    b3 = b2.reshape(CHUNK // SUBLANES, SUBLANES, n)
    r = lax.broadcasted_iota(jnp.int32, b3.shape, 1)
    mid = jnp.broadcast_to(b3[:, m:m + 1, :], b3.shape)
    for j in range(1, SUBLANES // blk):
        mid = jnp.where(r >= j * blk, jnp.broadcast_to(b3[:, j * blk + m:j * blk + m + 1, :], b3.shape), mid)
    return mid.reshape(CHUNK, n)


def _scan_kernel(lvl_ref, q_ref, k_ref, b_ref, v_ref, o_ref, st_ref, *, reverse):
    @pl.when(pl.program_id(1) == 0)
    def _():
        st_ref[...] = jnp.zeros(st_ref.shape, F32)

    pw = 2 * HG_DK
    lvl = lvl_ref[...]
    lane = lax.broadcasted_iota(jnp.int32, (1, pw), 1)
    same_head = (lax.broadcasted_iota(jnp.int32, (pw, pw), 0) < HG_DK) == (lane < HG_DK)
    zero = jnp.zeros((), BF16)
    edge = 0 if reverse else CHUNK - 1

    def pair_rows(x):
        z = jnp.zeros((CHUNK, HG_DK), x.dtype)
        return jnp.concatenate([jnp.concatenate([x[:, :HG_DK], z], axis=1),
                                jnp.concatenate([z, x[:, HG_DK:]], axis=1)], axis=0)

    for p in range(HG_HEADS // 2):
        cs = slice(p * pw, (p + 1) * pw)
        q2 = q_ref[:, cs]
        k2 = k_ref[:, cs]
        b2 = b_ref[:, cs]
        v2 = v_ref[:, cs]
        b_edge = b2[edge:edge + 1, :]
        st = st_ref[p]
        o = _dot_nt(q2 * jnp.exp2(b2).astype(BF16), st.astype(BF16))
        qk = q2.astype(F32) * k2.astype(F32)
        diag = jnp.where(lane < CHUNK, jnp.sum(qk[:, :HG_DK], axis=-1, keepdims=True),
                         jnp.sum(qk[:, HG_DK:], axis=-1, keepdims=True))
        a = jnp.where(lvl == N_LEVELS, diag.astype(BF16), zero)
        for l in range(N_LEVELS):
            e = jnp.exp2(_neg_abs(b2 - _mid_rows(b2, 2 ** l, reverse))).astype(BF16)
            a_l = _dot_nt(q2 * e, pair_rows(k2 * e))
            a = jnp.where(lvl == l, a_l.astype(BF16), a)
        o = o + _dot(a, pair_rows(v2))
        o_ref[:, cs] = o.astype(BF16)
        kdec = k2 * jnp.exp2(b_edge - b2).astype(BF16)
        upd = _dot_tn(v2, kdec)
        st_ref[p] = st * jnp.exp2(b_edge) + jnp.where(same_head, upd, 0.0)


def _hgrn_scan(lvl, q, k, bcum, v, seq, ctx, reverse):
    b, t, w = q.shape
    n_lat, n_ctx = seq // CHUNK, ctx // CHUNK
    n = n_lat + n_ctx

    def chunk(i):
        if reverse:
            return jnp.where(i < n_ctx, n - 1 - i, n - 1 - i)
        return jnp.where(i < n_ctx, n_lat + i, i - n_ctx)

    spec = pl.BlockSpec((None, CHUNK, w), lambda b, i: (b, chunk(i), 0))
    return pl.pallas_call(
        functools.partial(_scan_kernel, reverse=reverse),
        grid=(b, n),
        in_specs=[_resident(lvl.shape), spec, spec, spec, spec],
        out_specs=spec,
        out_shape=jax.ShapeDtypeStruct((b, t, w), BF16),
        scratch_shapes=[pltpu.VMEM((HG_HEADS // 2, 2 * HG_DK, 2 * HG_DK), F32)],
        compiler_params=_params(("parallel", "arbitrary")),
        name="hgrn_scan_bwd" if reverse else "hgrn_scan_fwd",
    )(lvl, q, k, bcum, v)


def _attn_out_kernel(x_ref, c_ref, mod_ref, o_ref, w_ref, y_ref, *, st):
    gate = mod_ref[...][:, 2 * st.d:3 * st.d]
    y_ref[...] = st.read(x_ref, c_ref) + gate * _dot(o_ref[...], w_ref[...])


def _attn_out(st, lat, ctx, mod, o, w):
    return pl.pallas_call(
        functools.partial(_attn_out_kernel, st=st),
        grid=st.grid,
        in_specs=[*st.sources(lat, ctx), st.mod(), st.tile(o.shape[-1]), _resident(w.shape)],
        out_specs=st.tile(st.d),
        out_shape=jax.ShapeDtypeStruct((st.batch, st.rows, st.d), F32),
        compiler_params=_params(("parallel", "parallel")),
        name="attn_out",
    )(lat, ctx, mod, o, w)


def _hgrn_out_kernel(x_ref, mod_ref, of_ref, ob_ref, gate_ref, og_ref, w_ref, y_ref, *, d):
    o = of_ref[...].astype(F32) + ob_ref[...].astype(F32)
    og = og_ref[...]
    parts = []
    for hd in range(HG_HEADS):
        oh = o[:, hd * HG_DK:(hd + 1) * HG_DK]
        ms = jnp.mean(oh * oh, axis=-1, keepdims=True)
        parts.append(oh * lax.rsqrt(ms + EPS) * og)
    r = jnp.concatenate(parts, axis=1) * gate_ref[...].astype(F32)
    gate = mod_ref[...][:, 2 * d:3 * d]
    y_ref[...] = x_ref[...] + gate * _dot(r.astype(BF16), w_ref[...])


def _hgrn_out(st, xs, mod, o_f, o_b, gate, o_gain, w):
    d = st.d
    return pl.pallas_call(
        functools.partial(_hgrn_out_kernel, d=d),
        grid=st.grid,
        in_specs=[st.tile(d), st.mod(), st.tile(d), st.tile(d), st.tile(d), st.row(HG_DK), _resident(w.shape)],
        out_specs=st.tile(d),
        out_shape=jax.ShapeDtypeStruct((st.batch, st.rows, d), F32),
        compiler_params=_params(("parallel", "parallel")),
        name="hgrn_out",
    )(xs, mod, o_f, o_b, gate, o_gain, w)


FFN_COLS = 256


def _ffn_kernel(x_ref, prev_ref, next_ref, mod_ref, gain_ref, wu_ref, cw_ref, cb_ref, wd_ref, y_ref, u_ref,
                *, d, n_lat, n_ctx):
    t = pl.program_id(1)
    has_prev = jnp.logical_and(t != 0, t != n_lat).astype(F32)
    has_next = jnp.logical_and(t != n_lat - 1, t != n_lat + n_ctx - 1).astype(F32)
    mod = mod_ref[...]
    gain, scale, shift = gain_ref[...], mod[:, 4 * d:5 * d], mod[:, 3 * d:4 * d]
    x = x_ref[...]
    h = _norm_mod(x, gain, scale, shift).astype(BF16)
    halo = jnp.concatenate([prev_ref[...], next_ref[...]], axis=0)
    keep = jnp.where(lax.broadcasted_iota(jnp.int32, (2 * SUBLANES, 1), 0) < SUBLANES, has_prev, has_next)
    h_ext = jnp.concatenate([h, (_norm_mod(halo, gain, scale, shift) * keep).astype(BF16)], axis=0)
    f = wd_ref.shape[0]
    row = lax.broadcasted_iota(jnp.int32, (TILE, 1), 0)
    for c in range(f // FFN_COLS):
        cs = slice(c * FFN_COLS, (c + 1) * FFN_COLS)
        ge = _dot(h_ext, wu_ref[:, cs])
        val = _dot(h, wu_ref[:, f + c * FFN_COLS:f + (c + 1) * FFN_COLS])
        g = ge[:TILE]
        before = ge[TILE + SUBLANES - 1:TILE + SUBLANES]
        after = ge[TILE + SUBLANES:TILE + SUBLANES + 1]
        up = jnp.where(row == 0, before, pltpu.roll(g, 1, axis=0))
        dn = jnp.where(row == TILE - 1, after, pltpu.roll(g, TILE - 1, axis=0))
        conv = up * cw_ref[0:1, cs] + g * cw_ref[1:2, cs] + dn * cw_ref[2:3, cs] + cb_ref[:, cs]
        u_ref[:, cs] = (_silu(conv) * val).astype(BF16)
    y_ref[...] = x + mod[:, 5 * d:6 * d] * _dot(u_ref[...], wd_ref[...])


def _ffn(st, xs, mod, gain, w_up, conv_w, conv_b, w_down):
    d = st.d
    f = w_down.shape[0]
    per_tile = TILE // SUBLANES
    last = st.rows // SUBLANES - 1
    return pl.pallas_call(
        functools.partial(_ffn_kernel, d=d, n_lat=st.n_lat, n_ctx=st.n_ctx),
        grid=st.grid,
        in_specs=[st.tile(d),
                  pl.BlockSpec((None, SUBLANES, d), lambda b, t: (b, jnp.maximum(t * per_tile - 1, 0), 0)),
                  pl.BlockSpec((None, SUBLANES, d), lambda b, t: (b, jnp.minimum((t + 1) * per_tile, last), 0)),
                  st.mod(), st.row(d), _resident(w_up.shape), _resident(conv_w.shape), _resident(conv_b.shape),
                  _resident(w_down.shape)],
        out_specs=st.tile(d),
        out_shape=jax.ShapeDtypeStruct((st.batch, st.rows, d), F32),
        scratch_shapes=[pltpu.VMEM((TILE, f), BF16)],
        compiler_params=_params(("parallel", "parallel")),
        name="ffn",
    )(xs, xs, xs, mod, gain, w_up, conv_w, conv_b, w_down)


def _rope_tables(seq, ctx):
    rows = seq // GRID_W
    row = jnp.repeat(jnp.arange(rows, dtype=F32), GRID_W)
    col = jnp.tile(jnp.arange(GRID_W, dtype=F32), rows)
    n_pairs = HEAD_DIM // 4
    inv = ROPE_THETA ** (-jnp.arange(n_pairs, dtype=F32) / n_pairs)
    ang_r, ang_c = row[:, None] * inv, col[:, None] * inv
    cos = jnp.concatenate([jnp.cos(ang_r)] * 2 + [jnp.cos(ang_c)] * 2, axis=1)
    sin = jnp.concatenate([-jnp.sin(ang_r), jnp.sin(ang_r), -jnp.sin(ang_c), jnp.sin(ang_c)], axis=1)
    cos = jnp.concatenate([cos, jnp.ones((ctx, HEAD_DIM), F32)], axis=0)
    sin = jnp.concatenate([sin, jnp.zeros((ctx, HEAD_DIM), F32)], axis=0)
    return jnp.tile(cos, (1, 4)), jnp.tile(sin, (1, 4))


def _head_block_constants():
    blk = 4 * HEAD_DIM
    i = np.arange(blk)
    gm = (i[:, None] // HEAD_DIM == i[None, :] // HEAD_DIM).astype(np.float32) / HEAD_DIM
    quarter = HEAD_DIM // 4
    partner = np.where((i % (2 * quarter)) < quarter, i + quarter, i - quarter)
    rot = (i[:, None] == partner[None, :]).astype(np.float32)
    return jnp.asarray(gm, BF16), jnp.asarray(rot, BF16)


def _query_head_order():
    return np.array([kh * GROUP + g for g in range(GROUP) for kh in range(N_KV_HEADS)])


def kernel(x, c, ctx, c_ctx, ada_w, ada_b, norm1_g, norm2_g, attn_w_in, attn_w_out, attn_q_gain, attn_k_gain,
           attn_sink, hgrn_w_in, hgrn_w_out, hgrn_o_gain, hgrn_lb_logits, ffn_w_up, ffn_conv_w, ffn_conv_b,
           ffn_w_down):
    batch, seq, d = x.shape
    n_ctx_tok = ctx.shape[1]
    depth = ada_w.shape[0]
    assert seq % TILE == 0 and n_ctx_tok % TILE == 0 and seq >= QBLK + 2 * WINDOW
    st = _Stream(batch, seq // TILE, n_ctx_tok // TILE, d)

    rows = -(-(batch + 1) // SUBLANES) * SUBLANES
    cc = jnp.zeros((rows, d), F32).at[:batch].set(c).at[batch].set(c_ctx)
    mod_all = _modulation(cc, ada_w, ada_b)[:, :batch + 1].reshape(depth, batch + 1, 1, 6 * d)

    cos, sin = _rope_tables(seq, n_ctx_tok)
    gm, rot = _head_block_constants()
    order = _query_head_order()
    nq = N_HEADS * HEAD_DIM
    lvl_f, lvl_b = jnp.asarray(_level_table(False), BF16), jnp.asarray(_level_table(True), BF16)

    st_lat = _Stream(batch, seq // TILE, 0, d)
    lat, con = x, ctx
    for layer in range(depth):
        j = layer // 2
        mod = mod_all[layer]
        g1 = norm1_g[layer].reshape(1, d)
        st_out = st_lat if layer == depth - 1 else st
        if layer % 2 == 0:
            w_in = attn_w_in[j]
            wq = w_in[:, :nq].reshape(d, N_HEADS, HEAD_DIM)[:, order].reshape(d, nq)
            w_in = jnp.concatenate([wq, w_in[:, nq:]], axis=1).astype(BF16)
            w_out = attn_w_out[j].reshape(N_HEADS, HEAD_DIM, d)[order].reshape(nq, d).astype(BF16)
            qg = jnp.tile(attn_q_gain[j], 4).reshape(1, 4 * HEAD_DIM)
            kg = jnp.tile(attn_k_gain[j], 4).reshape(1, 4 * HEAD_DIM)
            q, k, v = _attn_proj(st, lat, con, mod, g1, w_in, qg, kg, cos, sin, gm, rot)
            o = _attention(attn_sink[j][order], q, k, v, seq, n_ctx_tok)
            xs = _attn_out(st_out, lat, con, mod, o, w_out)
        else:
            q, kf, bf, kb, bb, v, gate = _hgrn_proj(st, lat, mod, g1, hgrn_w_in[j].astype(BF16), hgrn_lb_logits,
                                                    layer)
            o_f = _hgrn_scan(lvl_f, q, kf, bf, v, seq, n_ctx_tok, reverse=False)
            o_b = _hgrn_scan(lvl_b, q, kb, bb, v, seq, n_ctx_tok, reverse=True)
            xs = _hgrn_out(st_out, lat, mod, o_f, o_b, gate, hgrn_o_gain[j].reshape(1, HG_DK),
                           hgrn_w_out[j].astype(BF16))
        xs = _ffn(st_out, xs, mod, norm2_g[layer].reshape(1, d), ffn_w_up[layer].astype(BF16), ffn_conv_w[layer],
                  ffn_conv_b[layer].reshape(1, -1), ffn_w_down[layer].astype(BF16))
        lat = con = xs
    return xs
```

```python
import functools
import math

import jax
import jax.numpy as jnp
import numpy as np
from jax import lax
from jax.experimental import pallas as pl
from jax.experimental.pallas import tpu as pltpu

F32 = jnp.float32
BF16 = jnp.bfloat16

EPS = 1e-6
HEAD_DIM = 64
N_HEADS = 16
N_KV_HEADS = 4
GROUP = N_HEADS // N_KV_HEADS
WINDOW = 128
ROPE_THETA = 10000.0
GRID_W = 64
HG_HEADS = 8
HG_DK = 128
CHUNK = 128
N_LEVELS = 7
TILE = 256
QBLK = 128
LANES = 128
SUBLANES = 8
NEG_BIG = -1e30
LOG2E = math.log2(math.e)
VMEM_LIMIT = 56 * 1024 * 1024


def _silu(x):
    return x / (1.0 + jnp.exp(-x))


def _sigmoid(x):
    return 1.0 / (1.0 + jnp.exp(-x))


def _dot(a, b):
    return jnp.dot(a, b, preferred_element_type=F32)


def _dot_nt(a, b):
    return lax.dot_general(a, b, (((1,), (1,)), ((), ())), preferred_element_type=F32)


def _dot_tn(a, b):
    return lax.dot_general(a, b, (((0,), (0,)), ((), ())), preferred_element_type=F32)


def _norm_mod(x, gain, scale, shift):
    ms = jnp.mean(x * x, axis=-1, keepdims=True)
    return (x * lax.rsqrt(ms + EPS)) * gain * (1.0 + scale) + shift


def _params(sem):
    return pltpu.CompilerParams(dimension_semantics=sem, vmem_limit_bytes=VMEM_LIMIT)


def _resident(shape):
    nd = len(shape)
    return pl.BlockSpec(shape, lambda *_: (0,) * nd, pipeline_mode=pl.Buffered(1))


def _mod_kernel(cc_ref, w_ref, b_ref, o_ref):
    a = _silu(cc_ref[...]).astype(BF16)
    o_ref[...] = _dot(a, w_ref[...].astype(BF16)) + b_ref[...]


def _modulation(cc, ada_w, ada_b):
    depth, d, n = ada_w.shape
    r = cc.shape[0]
    tn = 1536
    return pl.pallas_call(
        _mod_kernel,
        grid=(depth, n // tn),
        in_specs=[
            pl.BlockSpec((r, d), lambda l, j: (0, 0)),
            pl.BlockSpec((None, d, tn), lambda l, j: (l, 0, j)),
            pl.BlockSpec((None, 1, tn), lambda l, j: (l, 0, j)),
        ],
        out_specs=pl.BlockSpec((None, r, tn), lambda l, j: (l, 0, j)),
        out_shape=jax.ShapeDtypeStruct((depth, r, n), F32),
        compiler_params=_params(("parallel", "parallel")),
        name="modulation",
    )(cc, ada_w, ada_b.reshape(depth, 1, n))


class _Stream:
    def __init__(self, batch, n_lat, n_ctx, d):
        self.batch, self.n_lat, self.n_ctx, self.d = batch, n_lat, n_ctx, d
        self.grid = (batch, n_lat + n_ctx)
        self.rows = (n_lat + n_ctx) * TILE

    def tile(self, width, col=0):
        return pl.BlockSpec((None, TILE, width), lambda b, t: (b, t, col))

    def sources(self, lat, ctx):
        n_lat = self.n_lat
        off = n_lat if ctx is lat else 0
        return [pl.BlockSpec((None, TILE, self.d), lambda b, t: (b, jnp.minimum(t, n_lat - 1), 0)),
                pl.BlockSpec((None, TILE, self.d), lambda b, t: (b, off + jnp.maximum(t - n_lat, 0), 0))]

    def read(self, lat_ref, ctx_ref):
        return jnp.where(pl.program_id(1) >= self.n_lat, ctx_ref[...], lat_ref[...])

    def mod(self):
        n_lat, batch = self.n_lat, self.batch
        return pl.BlockSpec((None, 1, 6 * self.d), lambda b, t: (jnp.where(t >= n_lat, batch, b), 0, 0))

    def row(self, width):
        return pl.BlockSpec((1, width), lambda b, t: (0, 0))


def _attn_proj_kernel(x_ref, c_ref, mod_ref, gain_ref, w_ref, qg_ref, kg_ref, cos_ref, sin_ref, gm_ref, rot_ref,
                      q_ref, k_ref, v_ref, *, st):
    d = st.d
    mod = mod_ref[...]
    h = _norm_mod(st.read(x_ref, c_ref), gain_ref[...], mod[:, d:2 * d], mod[:, 0:d]).astype(BF16)
    acc = _dot(h, w_ref[...])
    cos, sin = cos_ref[...], sin_ref[...]
    gm, rot = gm_ref[...], rot_ref[...]
    blk = 4 * HEAD_DIM
    n_qblk = N_HEADS * HEAD_DIM // blk
    for c in range(n_qblk + 1):
        xc = acc[:, c * blk:(c + 1) * blk]
        ms = _dot((xc * xc).astype(BF16), gm)
        gain = qg_ref[...] if c < n_qblk else kg_ref[...]
        xn = xc * lax.rsqrt(ms + EPS) * gain
        y = xn * cos + _dot(xn.astype(BF16), rot) * sin
        if c < n_qblk:
            q_ref[:, c * blk:(c + 1) * blk] = (y * (HEAD_DIM ** -0.5 * LOG2E)).astype(BF16)
        else:
            k_ref[...] = y.astype(BF16)
    v_ref[...] = acc[:, (n_qblk + 1) * blk:].astype(BF16)


def _attn_proj(st, lat, ctx, mod, gain, w, qg, kg, cos, sin, gm, rot):
    b, d = st.batch, st.d
    t = st.rows
    nq, nk = N_HEADS * HEAD_DIM, N_KV_HEADS * HEAD_DIM
    return pl.pallas_call(
        functools.partial(_attn_proj_kernel, st=st),
        grid=st.grid,
        in_specs=[*st.sources(lat, ctx), st.mod(), st.row(d), _resident(w.shape), st.row(nk), st.row(nk),
                  pl.BlockSpec((TILE, nk), lambda b, t: (t, 0)), pl.BlockSpec((TILE, nk), lambda b, t: (t, 0)),
                  _resident(gm.shape), _resident(rot.shape)],
        out_specs=[st.tile(nq), st.tile(nk), st.tile(nk)],
        out_shape=[jax.ShapeDtypeStruct((b, t, nq), BF16), jax.ShapeDtypeStruct((b, t, nk), BF16),
                   jax.ShapeDtypeStruct((b, t, nk), BF16)],
        compiler_params=_params(("parallel", "parallel")),
        name="attn_proj",
    )(lat, ctx, mod, gain, w, qg, kg, cos, sin, gm, rot)


def _attn_kernel(sink_ref, x_ref, c_ref, mod_ref, q_ref, k_ref, v_ref, w_ref, y_ref, o_ref, *, seq, ctx, d):
    n = pl.program_id(1)
    n_lat = seq // QBLK
    wlen = QBLK + 2 * WINDOW
    is_ctx = n >= n_lat
    ws = pl.multiple_of(jnp.clip(n * QBLK - WINDOW, 0, seq - wlen), QBLK)
    kall = jnp.concatenate([k_ref[pl.ds(ws, wlen), :], k_ref[seq:seq + ctx, :]], axis=0)
    vall = jnp.concatenate([v_ref[pl.ds(ws, wlen), :], v_ref[seq:seq + ctx, :]], axis=0)
    nkeys = wlen + ctx
    qpos = n * QBLK + lax.broadcasted_iota(jnp.int32, (QBLK, wlen), 0)
    kpos = ws + lax.broadcasted_iota(jnp.int32, (QBLK, wlen), 1)
    in_window = (jnp.abs(qpos - kpos) <= WINDOW) & jnp.logical_not(is_ctx)
    bias = jnp.where(in_window, 0.0, NEG_BIG).astype(F32)
    left = lax.broadcasted_iota(jnp.int32, (1, LANES), 1) < HEAD_DIM
    kv_of_lane = lax.broadcasted_iota(jnp.int32, (1, N_KV_HEADS * HEAD_DIM), 1) // HEAD_DIM
    zero = jnp.zeros((), BF16)
    kbd = [jnp.concatenate([jnp.where(left, kall[:, p * LANES:(p + 1) * LANES], zero),
                            jnp.where(left, zero, kall[:, p * LANES:(p + 1) * LANES])], axis=0)
           for p in range(N_KV_HEADS // 2)]
    vbd = jnp.concatenate([jnp.where(kv_of_lane == kh, vall, zero) for kh in range(N_KV_HEADS)], axis=0)
    blk = N_KV_HEADS * HEAD_DIM
    for g in range(GROUP):
        probs, invs = [], []
        for p in range(N_KV_HEADS // 2):
            q2 = q_ref[:, g * blk + p * LANES:g * blk + (p + 1) * LANES]
            s = _dot_nt(q2, kbd[p])
            for half in range(2):
                sink = sink_ref[g * N_KV_HEADS + 2 * p + half] * LOG2E
                sw = s[:, half * nkeys:half * nkeys + wlen] + bias
                sc = s[:, half * nkeys + wlen:(half + 1) * nkeys]
                m = jnp.maximum(jnp.maximum(jnp.max(sw, axis=-1, keepdims=True),
                                            jnp.max(sc, axis=-1, keepdims=True)), sink)
                pw, pc = jnp.exp2(sw - m), jnp.exp2(sc - m)
                den = jnp.sum(pw, axis=-1, keepdims=True) + jnp.sum(pc, axis=-1, keepdims=True) + jnp.exp2(sink - m)
                probs += [pw.astype(BF16), pc.astype(BF16)]
                invs.append(1.0 / den)
        o = _dot(jnp.concatenate(probs, axis=1), vbd)
        inv = jnp.where(kv_of_lane == 0, invs[0], jnp.where(kv_of_lane == 1, invs[1],
                                                            jnp.where(kv_of_lane == 2, invs[2], invs[3])))
        o_ref[:, g * blk:(g + 1) * blk] = (o * inv).astype(BF16)
    x = jnp.where(is_ctx, c_ref[...], x_ref[...])
    y_ref[...] = x + mod_ref[...][:, 2 * d:3 * d] * _dot(o_ref[...], w_ref[...])


def _attention(sink, lat, con, mod, q, k, v, w_out, seq, ctx):
    b, t, nq = q.shape
    nk = k.shape[-1]
    d = w_out.shape[1]
    n_lat = seq // QBLK
    off = n_lat if con is lat else 0
    return pl.pallas_call(
        functools.partial(_attn_kernel, seq=seq, ctx=ctx, d=d),
        grid=(b, t // QBLK),
        in_specs=[pl.BlockSpec(memory_space=pltpu.SMEM),
                  pl.BlockSpec((None, QBLK, d), lambda b, n: (b, jnp.minimum(n, n_lat - 1), 0)),
                  pl.BlockSpec((None, QBLK, d), lambda b, n: (b, off + jnp.maximum(n - n_lat, 0), 0)),
                  pl.BlockSpec((None, 1, 6 * d), lambda b, n: (jnp.where(n >= n_lat, mod.shape[0] - 1, b), 0, 0)),
                  pl.BlockSpec((None, QBLK, nq), lambda b, n: (b, n, 0)),
                  pl.BlockSpec((None, t, nk), lambda b, n: (b, 0, 0)),
                  pl.BlockSpec((None, t, nk), lambda b, n: (b, 0, 0)),
                  _resident(w_out.shape)],
        out_specs=pl.BlockSpec((None, QBLK, d), lambda b, n: (b, n, 0)),
        out_shape=jax.ShapeDtypeStruct((b, t, d), F32),
        scratch_shapes=[pltpu.VMEM((QBLK, nq), BF16)],
        compiler_params=_params(("parallel", "parallel")),
        name="attention",
    )(sink, lat, con, mod, q, k, v, w_out)


PROJ_COLS = 256


def _chunk_cumsum(g, reverse):
    n = g.shape[-1]
    groups = CHUNK // SUBLANES
    g = g.reshape(groups, SUBLANES, n)
    r = lax.broadcasted_iota(jnp.int32, g.shape, 1)
    d = 1
    while d < SUBLANES:
        if reverse:
            g = g + jnp.where(r < SUBLANES - d, pltpu.roll(g, SUBLANES - d, axis=1), 0.0)
        else:
            g = g + jnp.where(r >= d, pltpu.roll(g, d, axis=1), 0.0)
        d *= 2
    g = g.reshape(CHUNK, n)
    edge = 0 if reverse else SUBLANES - 1
    parts = [None] * groups
    carry = None
    for j in (reversed(range(groups)) if reverse else range(groups)):
        blk = g[j * SUBLANES:(j + 1) * SUBLANES]
        if carry is not None:
            blk = blk + carry
        parts[j] = blk
        carry = blk[edge:edge + 1]
    return jnp.concatenate(parts, axis=0)


def _hgrn_proj_kernel(x_ref, mod_ref, gain_ref, w_ref, lbl_ref, q_ref, kf_ref, bf_ref, kb_ref, bb_ref, v_ref,
                      gate_ref, acc_ref, *, d, layer):
    mod = mod_ref[...]
    h = _norm_mod(x_ref[...], gain_ref[...], mod[:, d:2 * d], mod[:, 0:d]).astype(BF16)
    lg = lbl_ref[...]
    e = jnp.exp(lg - jnp.max(lg, axis=0, keepdims=True))
    lb = jnp.sum(e[1:layer + 1], axis=0, keepdims=True) / jnp.sum(e, axis=0, keepdims=True)
    w = HG_HEADS * HG_DK

    def finish(seg, c, acc):
        cs = slice(c * PROJ_COLS, (c + 1) * PROJ_COLS)
        if seg == 0:
            q_ref[:, cs] = _silu(acc).astype(BF16)
        elif seg in (1, 2):
            k_ref, b_ref = (kf_ref, bf_ref) if seg == 1 else (kb_ref, bb_ref)
            f = lb[:, cs] + (1.0 - lb[:, cs]) * _sigmoid(acc)
            k_ref[:, cs] = (1.0 - f).astype(BF16)
            g = jnp.log2(f)
            for r in range(TILE // CHUNK):
                rs = slice(r * CHUNK, (r + 1) * CHUNK)
                b_ref[rs, cs] = _chunk_cumsum(g[rs], reverse=seg == 2)
        elif seg == 3:
            v_ref[:, cs] = acc.astype(BF16)
        else:
            gate_ref[:, cs] = _silu(acc).astype(BF16)

    base = jnp.minimum(pl.program_id(0), 0)
    items = [(seg, c) for c in range(w // PROJ_COLS) for seg in (1, 2, 0, 4, 3)]
    pending = None
    for i, (seg, c) in enumerate(items):
        acc_ref[base + i % 2] = _dot(h, w_ref[:, seg * w + c * PROJ_COLS:seg * w + (c + 1) * PROJ_COLS])
        if pending is not None:
            finish(*pending[:2], acc_ref[base + pending[2]])
        pending = (seg, c, i % 2)
    finish(*pending[:2], acc_ref[base + pending[2]])


def _hgrn_proj(st, xs, mod, gain, w, lb_logits, layer):
    b, t, d = xs.shape
    wd = HG_HEADS * HG_DK
    assert wd == d
    shapes = [(wd, BF16), (wd, BF16), (wd, F32), (wd, BF16), (wd, F32), (d, BF16), (d, BF16)]
    return pl.pallas_call(
        functools.partial(_hgrn_proj_kernel, d=d, layer=layer),
        grid=st.grid,
        in_specs=[st.tile(d), st.mod(), st.row(d), _resident(w.shape), _resident(lb_logits.shape)],
        out_specs=[st.tile(n) for n, _ in shapes],
        out_shape=[jax.ShapeDtypeStruct((b, t, n), dt) for n, dt in shapes],
        scratch_shapes=[pltpu.VMEM((2, TILE, PROJ_COLS), F32)],
        compiler_params=_params(("parallel", "parallel")),
        name="hgrn_proj",
    )(xs, mod, gain, w, lb_logits)


def _level_table(reverse):
    t = np.arange(CHUNK)[:, None]
    s = np.arange(CHUNK)[None, :]
    x = t ^ s
    lvl = np.where(x > 0, np.floor(np.log2(np.maximum(x, 1))).astype(np.int32), N_LEVELS)
    later = (t < s) if reverse else (t > s)
    lvl = np.where(later | (t == s), lvl, -1).astype(np.float32)
    return np.concatenate([lvl, lvl], axis=1)


def _neg_abs(x):
    sign = jnp.int32(-2 ** 31)
    return lax.bitcast_convert_type(lax.bitcast_convert_type(x, jnp.int32) | sign, F32)


def _mid_rows(b2, half, reverse):
    n = b2.shape[-1]
    blk = 2 * half
    m = half if reverse else half - 1
    if half == 1:
        b3 = b2.reshape(CHUNK // SUBLANES, SUBLANES, n)
        odd = lax.broadcasted_iota(jnp.int32, b3.shape, 1) % 2 == 1
        if reverse:
            return jnp.where(odd, b3, pltpu.roll(b3, SUBLANES - 1, axis=1)).reshape(CHUNK, n)
        return jnp.where(odd, pltpu.roll(b3, 1, axis=1), b3).reshape(CHUNK, n)
    if blk >= SUBLANES:
        b3 = b2.reshape(CHUNK // blk, blk, n)
        return jnp.broadcast_to(b3[:, m:m + 1, :], b3.shape).reshape(CHUNK, n)
    b3 = b2.reshape(CHUNK // SUBLANES, SUBLANES, n)
    r = lax.broadcasted_iota(jnp.int32, b3.shape, 1)
    mid = jnp.broadcast_to(b3[:, m:m + 1, :], b3.shape)
    for j in range(1, SUBLANES // blk):
        mid = jnp.where(r >= j * blk, jnp.broadcast_to(b3[:, j * blk + m:j * blk + m + 1, :], b3.shape), mid)
    return mid.reshape(CHUNK, n)


def _scan_chunk(lvl_ref, q_ref, k_ref, b_ref, v_ref, o_ref, st_ref, reverse):
    @pl.when(pl.program_id(1) == 0)
    def _():
        st_ref[...] = jnp.zeros(st_ref.shape, F32)

    pw = 2 * HG_DK
    lvl = lvl_ref[...]
    lane = lax.broadcasted_iota(jnp.int32, (1, pw), 1)
    same_head = (lax.broadcasted_iota(jnp.int32, (pw, pw), 0) < HG_DK) == (lane < HG_DK)
    zero = jnp.zeros((), BF16)
    edge = 0 if reverse else CHUNK - 1

    def pair_rows(x):
        z = jnp.zeros((CHUNK, HG_DK), x.dtype)
        return jnp.concatenate([jnp.concatenate([x[:, :HG_DK], z], axis=1),
                                jnp.concatenate([z, x[:, HG_DK:]], axis=1)], axis=0)

    def pair_cols(xt):
        z = jnp.zeros((HG_DK, CHUNK), xt.dtype)
        return jnp.concatenate([jnp.concatenate([xt[:HG_DK], z], axis=1),
                                jnp.concatenate([z, xt[HG_DK:]], axis=1)], axis=0)

    for r, p in [(r, p) for r in range(q_ref.shape[0]) for p in range(HG_HEADS // 2)]:
        cs = slice(p * pw, (p + 1) * pw)
        q2 = q_ref[r, :, cs]
        k2 = k_ref[r, :, cs]
        b2 = b_ref[r, :, cs]
        v2 = v_ref[r, :, cs]
        b_edge = b2[edge:edge + 1, :]
        st = st_ref[r, p]
        o = _dot_nt(q2 * jnp.exp2(b2).astype(BF16), st.astype(BF16))
        qk = q2.astype(F32) * k2.astype(F32)
        diag = jnp.where(lane < CHUNK, jnp.sum(qk[:, :HG_DK], axis=-1, keepdims=True),
                         jnp.sum(qk[:, HG_DK:], axis=-1, keepdims=True))
        a = jnp.where(lvl == N_LEVELS, diag.astype(BF16), zero)
        kt = k2.T
        for l in range(N_LEVELS):
            e = jnp.exp2(_neg_abs(b2 - _mid_rows(b2, 2 ** l, reverse))).astype(BF16)
            a_l = _dot(q2 * e, pair_cols(kt * e.T))
            a = jnp.where(lvl == l, a_l.astype(BF16), a)
        o = o + _dot(a, pair_rows(v2))
        o_ref[r, :, cs] = o.astype(o_ref.dtype)
        kdec = k2 * jnp.exp2(b_edge - b2).astype(BF16)
        upd = _dot_tn(v2, kdec)
        st_ref[r, p] = st * jnp.exp2(b_edge) + jnp.where(same_head, upd, 0.0)


def _scan_fwd_kernel(lvl_ref, q_ref, k_ref, b_ref, v_ref, o_ref, st_ref):
    _scan_chunk(lvl_ref, q_ref, k_ref, b_ref, v_ref, o_ref, st_ref, reverse=False)


def _scan_bwd_out_kernel(lvl_ref, q_ref, k_ref, b_ref, v_ref, of_ref, gate_ref, x_ref, modl_ref, modc_ref, og_ref,
                         w_ref, y_ref, st_ref, ob_ref, *, n_ctx, d):
    _scan_chunk(lvl_ref, q_ref, k_ref, b_ref, v_ref, ob_ref, st_ref, reverse=True)
    rows = q_ref.shape[0]
    og = og_ref[...]
    parts = []
    for r in range(rows):
        o = of_ref[r].astype(F32) + ob_ref[r]
        heads = []
        for hd in range(HG_HEADS):
            oh = o[:, hd * HG_DK:(hd + 1) * HG_DK]
            ms = jnp.mean(oh * oh, axis=-1, keepdims=True)
            heads.append(oh * lax.rsqrt(ms + EPS) * og)
        parts.append((jnp.concatenate(heads, axis=1) * gate_ref[r].astype(F32)).astype(BF16))
    acc = _dot(jnp.concatenate(parts, axis=0), w_ref[...])
    is_ctx = pl.program_id(1) < n_ctx
    for r in range(rows):
        g1 = jnp.where(is_ctx, modc_ref[...], modl_ref[r])[:, 2 * d:3 * d]
        y_ref[r] = x_ref[r] + g1 * acc[r * CHUNK:(r + 1) * CHUNK]


SCAN_ROWS = 2


def _scan_specs(b, w, seq, ctx, reverse):
    n_lat, n_ctx = seq // CHUNK, ctx // CHUNK
    n = n_lat + n_ctx

    def chunk(i):
        if reverse:
            return n - 1 - i
        return jnp.where(i < n_ctx, n_lat + i, i - n_ctx)

    rows = SCAN_ROWS if b % SCAN_ROWS == 0 else 1
    return rows, n, pl.BlockSpec((rows, CHUNK, w), lambda b, i: (b, chunk(i), 0))


def _hgrn_scan_fwd(lvl, q, k, bcum, v, seq, ctx):
    b, t, w = q.shape
    rows, n, spec = _scan_specs(b, w, seq, ctx, reverse=False)
    return pl.pallas_call(
        _scan_fwd_kernel,
        grid=(b // rows, n),
        in_specs=[_resident(lvl.shape), spec, spec, spec, spec],
        out_specs=spec,
        out_shape=jax.ShapeDtypeStruct((b, t, w), BF16),
        scratch_shapes=[pltpu.VMEM((rows, HG_HEADS // 2, 2 * HG_DK, 2 * HG_DK), F32)],
        compiler_params=_params(("parallel", "arbitrary")),
        name="hgrn_scan_fwd",
    )(lvl, q, k, bcum, v)


def _hgrn_scan_bwd_out(lvl, q, k, bcum, v, o_f, gate, xs, mod, o_gain, w_out, seq, ctx):
    b, t, w = q.shape
    d = w_out.shape[1]
    assert w == d
    rows, n, spec = _scan_specs(b, w, seq, ctx, reverse=True)
    return pl.pallas_call(
        functools.partial(_scan_bwd_out_kernel, n_ctx=ctx // CHUNK, d=d),
        grid=(b // rows, n),
        in_specs=[_resident(lvl.shape), spec, spec, spec, spec, spec, spec, spec,
                  pl.BlockSpec((rows, 1, 6 * d), lambda b, i: (b, 0, 0)),
                  pl.BlockSpec((None, 1, 6 * d), lambda b, i: (mod.shape[0] - 1, 0, 0)),
                  pl.BlockSpec((1, HG_DK), lambda b, i: (0, 0)), _resident(w_out.shape)],
        out_specs=spec,
        out_shape=jax.ShapeDtypeStruct((b, t, d), F32),
        scratch_shapes=[pltpu.VMEM((rows, HG_HEADS // 2, 2 * HG_DK, 2 * HG_DK), F32),
                        pltpu.VMEM((rows, CHUNK, w), F32)],
        compiler_params=_params(("parallel", "arbitrary")),
        name="hgrn_scan_bwd_out",
    )(lvl, q, k, bcum, v, o_f, gate, xs, mod, mod, o_gain, w_out)


FFN_COLS = 256


def _ffn_kernel(x_ref, prev_ref, next_ref, mod_ref, gain_ref, wu_ref, cw_ref, cb_ref, wd_ref, y_ref, u_ref,
                *, d, n_lat, n_ctx):
    t = pl.program_id(1)
    has_prev = jnp.logical_and(t != 0, t != n_lat).astype(F32)
    has_next = jnp.logical_and(t != n_lat - 1, t != n_lat + n_ctx - 1).astype(F32)
    mod = mod_ref[...]
    gain, scale, shift = gain_ref[...], mod[:, 4 * d:5 * d], mod[:, 3 * d:4 * d]
    x = x_ref[...]
    h = _norm_mod(x, gain, scale, shift).astype(BF16)
    halo = jnp.concatenate([prev_ref[...], next_ref[...]], axis=0)
    keep = jnp.where(lax.broadcasted_iota(jnp.int32, (2 * SUBLANES, 1), 0) < SUBLANES, has_prev, has_next)
    h_ext = jnp.concatenate([h, (_norm_mod(halo, gain, scale, shift) * keep).astype(BF16)], axis=0)
    f = wd_ref.shape[0]
    row = lax.broadcasted_iota(jnp.int32, (TILE, 1), 0)
    for c in range(f // FFN_COLS):
        cs = slice(c * FFN_COLS, (c + 1) * FFN_COLS)
        ge = _dot(h_ext, wu_ref[:, cs])
        val = _dot(h, wu_ref[:, f + c * FFN_COLS:f + (c + 1) * FFN_COLS])
        g = ge[:TILE]
        before = ge[TILE + SUBLANES - 1:TILE + SUBLANES]
        after = ge[TILE + SUBLANES:TILE + SUBLANES + 1]
        up = jnp.where(row == 0, before, pltpu.roll(g, 1, axis=0))
        dn = jnp.where(row == TILE - 1, after, pltpu.roll(g, TILE - 1, axis=0))
        conv = up * cw_ref[0:1, cs] + g * cw_ref[1:2, cs] + dn * cw_ref[2:3, cs] + cb_ref[:, cs]
        u_ref[:, cs] = (_silu(conv) * val).astype(BF16)
    y_ref[...] = x + mod[:, 5 * d:6 * d] * _dot(u_ref[...], wd_ref[...])


def _ffn(st, xs, mod, gain, w_up, conv_w, conv_b, w_down):
    d = st.d
    f = w_down.shape[0]
    per_tile = TILE // SUBLANES
    last = st.rows // SUBLANES - 1
    return pl.pallas_call(
        functools.partial(_ffn_kernel, d=d, n_lat=st.n_lat, n_ctx=st.n_ctx),
        grid=st.grid,
        in_specs=[st.tile(d),
                  pl.BlockSpec((None, SUBLANES, d), lambda b, t: (b, jnp.maximum(t * per_tile - 1, 0), 0)),
                  pl.BlockSpec((None, SUBLANES, d), lambda b, t: (b, jnp.minimum((t + 1) * per_tile, last), 0)),
                  st.mod(), st.row(d), _resident(w_up.shape), _resident(conv_w.shape), _resident(conv_b.shape),
                  _resident(w_down.shape)],
        out_specs=st.tile(d),
        out_shape=jax.ShapeDtypeStruct((st.batch, st.rows, d), F32),
        scratch_shapes=[pltpu.VMEM((TILE, f), BF16)],
        compiler_params=_params(("parallel", "parallel")),
        name="ffn",
    )(xs, xs, xs, mod, gain, w_up, conv_w, conv_b, w_down)


def _rope_tables(seq, ctx):
    rows = seq // GRID_W
    row = jnp.repeat(jnp.arange(rows, dtype=F32), GRID_W)
    col = jnp.tile(jnp.arange(GRID_W, dtype=F32), rows)
    n_pairs = HEAD_DIM // 4
    inv = ROPE_THETA ** (-jnp.arange(n_pairs, dtype=F32) / n_pairs)
    ang_r, ang_c = row[:, None] * inv, col[:, None] * inv
    cos = jnp.concatenate([jnp.cos(ang_r)] * 2 + [jnp.cos(ang_c)] * 2, axis=1)
    sin = jnp.concatenate([-jnp.sin(ang_r), jnp.sin(ang_r), -jnp.sin(ang_c), jnp.sin(ang_c)], axis=1)
    cos = jnp.concatenate([cos, jnp.ones((ctx, HEAD_DIM), F32)], axis=0)
    sin = jnp.concatenate([sin, jnp.zeros((ctx, HEAD_DIM), F32)], axis=0)
    return jnp.tile(cos, (1, 4)), jnp.tile(sin, (1, 4))


def _head_block_constants():
    blk = 4 * HEAD_DIM
    i = np.arange(blk)
    gm = (i[:, None] // HEAD_DIM == i[None, :] // HEAD_DIM).astype(np.float32) / HEAD_DIM
    quarter = HEAD_DIM // 4
    partner = np.where((i % (2 * quarter)) < quarter, i + quarter, i - quarter)
    rot = (i[:, None] == partner[None, :]).astype(np.float32)
    return jnp.asarray(gm, BF16), jnp.asarray(rot, BF16)


def _query_head_order():
    return np.array([kh * GROUP + g for g in range(GROUP) for kh in range(N_KV_HEADS)])


def kernel(x, c, ctx, c_ctx, ada_w, ada_b, norm1_g, norm2_g, attn_w_in, attn_w_out, attn_q_gain, attn_k_gain,
           attn_sink, hgrn_w_in, hgrn_w_out, hgrn_o_gain, hgrn_lb_logits, ffn_w_up, ffn_conv_w, ffn_conv_b,
           ffn_w_down):
    batch, seq, d = x.shape
    n_ctx_tok = ctx.shape[1]
    depth = ada_w.shape[0]
    assert seq % TILE == 0 and n_ctx_tok % TILE == 0 and seq >= QBLK + 2 * WINDOW
    st = _Stream(batch, seq // TILE, n_ctx_tok // TILE, d)

    rows = -(-(batch + 1) // SUBLANES) * SUBLANES
    cc = jnp.zeros((rows, d), F32).at[:batch].set(c).at[batch].set(c_ctx)
    mod_all = _modulation(cc, ada_w, ada_b)[:, :batch + 1].reshape(depth, batch + 1, 1, 6 * d)

    cos, sin = _rope_tables(seq, n_ctx_tok)
    gm, rot = _head_block_constants()
    order = _query_head_order()
    nq = N_HEADS * HEAD_DIM
    lvl_f, lvl_b = jnp.asarray(_level_table(False), BF16), jnp.asarray(_level_table(True), BF16)

    st_lat = _Stream(batch, seq // TILE, 0, d)
    lat, con = x, ctx
    for layer in range(depth):
        j = layer // 2
        mod = mod_all[layer]
        g1 = norm1_g[layer].reshape(1, d)
        st_out = st_lat if layer == depth - 1 else st
        if layer % 2 == 0:
            w_in = attn_w_in[j]
            wq = w_in[:, :nq].reshape(d, N_HEADS, HEAD_DIM)[:, order].reshape(d, nq)
            w_in = jnp.concatenate([wq, w_in[:, nq:]], axis=1).astype(BF16)
            w_out = attn_w_out[j].reshape(N_HEADS, HEAD_DIM, d)[order].reshape(nq, d).astype(BF16)
            qg = jnp.tile(attn_q_gain[j], 4).reshape(1, 4 * HEAD_DIM)
            kg = jnp.tile(attn_k_gain[j], 4).reshape(1, 4 * HEAD_DIM)
            q, k, v = _attn_proj(st, lat, con, mod, g1, w_in, qg, kg, cos, sin, gm, rot)
            xs = _attention(attn_sink[j][order], lat, con, mod, q, k, v, w_out, seq, n_ctx_tok)
        else:
            q, kf, bf, kb, bb, v, gate = _hgrn_proj(st, lat, mod, g1, hgrn_w_in[j].astype(BF16), hgrn_lb_logits,
                                                    layer)
            o_f = _hgrn_scan_fwd(lvl_f, q, kf, bf, v, seq, n_ctx_tok)
            xs = _hgrn_scan_bwd_out(lvl_b, q, kb, bb, v, o_f, gate, lat, mod, hgrn_o_gain[j].reshape(1, HG_DK),
                                    hgrn_w_out[j].astype(BF16), seq, n_ctx_tok)
        xs = _ffn(st_out, xs, mod, norm2_g[layer].reshape(1, d), ffn_w_up[layer].astype(BF16), ffn_conv_w[layer],
                  ffn_conv_b[layer].reshape(1, -1), ffn_w_down[layer].astype(BF16))
        lat = con = xs
    return xs
```

```python
import functools
import math

import jax
import jax.numpy as jnp
import numpy as np
from jax import lax
from jax.experimental import pallas as pl
from jax.experimental.pallas import tpu as pltpu

F32 = jnp.float32
BF16 = jnp.bfloat16

EPS = 1e-6
HEAD_DIM = 64
N_HEADS = 16
N_KV_HEADS = 4
GROUP = N_HEADS // N_KV_HEADS
WINDOW = 128
ROPE_THETA = 10000.0
GRID_W = 64
HG_HEADS = 8
HG_DK = 128
CHUNK = 128
N_LEVELS = 7
TILE = 256
STREAM_ROWS = 2
QBLK = 128
LANES = 128
SUBLANES = 8
NEG_BIG = -1e30
LOG2E = math.log2(math.e)
VMEM_LIMIT = 56 * 1024 * 1024


def _silu(x):
    return x / (1.0 + jnp.exp(-x))


def _sigmoid(x):
    return 1.0 / (1.0 + jnp.exp(-x))


def _dot(a, b):
    return jnp.dot(a, b, preferred_element_type=F32)


def _dot_nt(a, b):
    return lax.dot_general(a, b, (((1,), (1,)), ((), ())), preferred_element_type=F32)


def _dot_tn(a, b):
    return lax.dot_general(a, b, (((0,), (0,)), ((), ())), preferred_element_type=F32)


def _norm_mod(x, gain, scale, shift):
    ms = jnp.mean(x * x, axis=-1, keepdims=True)
    return (x * lax.rsqrt(ms + EPS)) * gain * (1.0 + scale) + shift


def _params(sem):
    return pltpu.CompilerParams(dimension_semantics=sem, vmem_limit_bytes=VMEM_LIMIT)


def _resident(shape):
    nd = len(shape)
    return pl.BlockSpec(shape, lambda *_: (0,) * nd, pipeline_mode=pl.Buffered(1))


def _mod_kernel(cc_ref, w_ref, b_ref, o_ref):
    a = _silu(cc_ref[...]).astype(BF16)
    o_ref[...] = _dot(a, w_ref[...].astype(BF16)) + b_ref[...]


def _modulation(cc, ada_w, ada_b):
    depth, d, n = ada_w.shape
    r = cc.shape[0]
    tn = 1536
    return pl.pallas_call(
        _mod_kernel,
        grid=(depth, n // tn),
        in_specs=[
            pl.BlockSpec((r, d), lambda l, j: (0, 0)),
            pl.BlockSpec((None, d, tn), lambda l, j: (l, 0, j)),
            pl.BlockSpec((None, 1, tn), lambda l, j: (l, 0, j)),
        ],
        out_specs=pl.BlockSpec((None, r, tn), lambda l, j: (l, 0, j)),
        out_shape=jax.ShapeDtypeStruct((depth, r, n), F32),
        compiler_params=_params(("parallel", "parallel")),
        name="modulation",
    )(cc, ada_w, ada_b.reshape(depth, 1, n))


class _Stream:
    def __init__(self, batch, n_lat, n_ctx, d):
        self.batch, self.n_lat, self.n_ctx, self.d = batch, n_lat, n_ctx, d
        self.nb = STREAM_ROWS if batch % STREAM_ROWS == 0 else 1
        self.grid = (batch // self.nb, n_lat + n_ctx)
        self.rows = (n_lat + n_ctx) * TILE

    def tile(self, width, col=0):
        return pl.BlockSpec((self.nb, TILE, width), lambda b, t: (b, t, col))

    def sources(self, lat, ctx):
        n_lat = self.n_lat
        off = n_lat if ctx is lat else 0
        return [pl.BlockSpec((self.nb, TILE, self.d), lambda b, t: (b, jnp.minimum(t, n_lat - 1), 0)),
                pl.BlockSpec((self.nb, TILE, self.d), lambda b, t: (b, off + jnp.maximum(t - n_lat, 0), 0))]

    def is_ctx(self):
        return pl.program_id(1) >= self.n_lat

    def read(self, lat_ref, ctx_ref):
        return [jnp.where(self.is_ctx(), ctx_ref[r], lat_ref[r]) for r in range(self.nb)]

    def mod(self):
        return [pl.BlockSpec((self.nb, 1, 6 * self.d), lambda b, t: (b, 0, 0)),
                pl.BlockSpec((None, 1, 6 * self.d), lambda b, t: (self.batch, 0, 0))]

    def read_mod(self, lat_ref, ctx_ref):
        return [jnp.where(self.is_ctx(), ctx_ref[...], lat_ref[r]) for r in range(self.nb)]

    def row(self, width):
        return pl.BlockSpec((1, width), lambda b, t: (0, 0))


def _attn_proj_kernel(x_ref, c_ref, modl_ref, modc_ref, gain_ref, w_ref, qg_ref, kg_ref, cos_ref, sin_ref, gm_ref,
                      rot_ref, q_ref, k_ref, v_ref, *, st):
    d, nb = st.d, st.nb
    mods = st.read_mod(modl_ref, modc_ref)
    h = jnp.concatenate([_norm_mod(x, gain_ref[...], m[:, d:2 * d], m[:, 0:d]).astype(BF16)
                         for x, m in zip(st.read(x_ref, c_ref), mods)], axis=0)
    acc = _dot(h, w_ref[...])
    cos = jnp.concatenate([cos_ref[...]] * nb, axis=0)
    sin = jnp.concatenate([sin_ref[...]] * nb, axis=0)
    gm, rot = gm_ref[...], rot_ref[...]
    blk = 4 * HEAD_DIM
    n_qblk = N_HEADS * HEAD_DIM // blk

    def put(ref, cs, val):
        for r in range(nb):
            ref[r, :, cs] = val[r * TILE:(r + 1) * TILE]

    for c in range(n_qblk + 1):
        xc = acc[:, c * blk:(c + 1) * blk]
        ms = _dot((xc * xc).astype(BF16), gm)
        gain = qg_ref[...] if c < n_qblk else kg_ref[...]
        xn = xc * lax.rsqrt(ms + EPS) * gain
        y = xn * cos + _dot(xn.astype(BF16), rot) * sin
        if c < n_qblk:
            put(q_ref, slice(c * blk, (c + 1) * blk), (y * (HEAD_DIM ** -0.5 * LOG2E)).astype(BF16))
        else:
            put(k_ref, slice(None), y.astype(BF16))
    put(v_ref, slice(None), acc[:, (n_qblk + 1) * blk:].astype(BF16))


def _attn_proj(st, lat, ctx, mod, gain, w, qg, kg, cos, sin, gm, rot):
    b, d = st.batch, st.d
    t = st.rows
    nq, nk = N_HEADS * HEAD_DIM, N_KV_HEADS * HEAD_DIM
    return pl.pallas_call(
        functools.partial(_attn_proj_kernel, st=st),
        grid=st.grid,
        in_specs=[*st.sources(lat, ctx), *st.mod(), st.row(d), _resident(w.shape), st.row(nk), st.row(nk),
                  pl.BlockSpec((TILE, nk), lambda b, t: (t, 0)), pl.BlockSpec((TILE, nk), lambda b, t: (t, 0)),
                  _resident(gm.shape), _resident(rot.shape)],
        out_specs=[st.tile(nq), st.tile(nk), st.tile(nk)],
        out_shape=[jax.ShapeDtypeStruct((b, t, nq), BF16), jax.ShapeDtypeStruct((b, t, nk), BF16),
                   jax.ShapeDtypeStruct((b, t, nk), BF16)],
        compiler_params=_params(("parallel", "parallel")),
        name="attn_proj",
    )(lat, ctx, mod, mod, gain, w, qg, kg, cos, sin, gm, rot)


def _attn_kernel(sink_ref, x_ref, c_ref, mod_ref, q_ref, k_ref, v_ref, w_ref, y_ref, o_ref, *, seq, ctx, d):
    n = pl.program_id(1)
    n_lat = seq // QBLK
    wlen = QBLK + 2 * WINDOW
    is_ctx = n >= n_lat
    ws = pl.multiple_of(jnp.clip(n * QBLK - WINDOW, 0, seq - wlen), QBLK)
    kall = jnp.concatenate([k_ref[pl.ds(ws, wlen), :], k_ref[seq:seq + ctx, :]], axis=0)
    vall = jnp.concatenate([v_ref[pl.ds(ws, wlen), :], v_ref[seq:seq + ctx, :]], axis=0)
    nkeys = wlen + ctx
    qpos = n * QBLK + lax.broadcasted_iota(jnp.int32, (QBLK, wlen), 0)
    kpos = ws + lax.broadcasted_iota(jnp.int32, (QBLK, wlen), 1)
    in_window = (jnp.abs(qpos - kpos) <= WINDOW) & jnp.logical_not(is_ctx)
    bias = jnp.where(in_window, 0.0, NEG_BIG).astype(F32)
    left = lax.broadcasted_iota(jnp.int32, (1, LANES), 1) < HEAD_DIM
    kv_of_lane = lax.broadcasted_iota(jnp.int32, (1, N_KV_HEADS * HEAD_DIM), 1) // HEAD_DIM
    zero = jnp.zeros((), BF16)
    kbd = [jnp.concatenate([jnp.where(left, kall[:, p * LANES:(p + 1) * LANES], zero),
                            jnp.where(left, zero, kall[:, p * LANES:(p + 1) * LANES])], axis=0)
           for p in range(N_KV_HEADS // 2)]
    vbd = jnp.concatenate([jnp.where(kv_of_lane == kh, vall, zero) for kh in range(N_KV_HEADS)], axis=0)
    blk = N_KV_HEADS * HEAD_DIM
    for g in range(GROUP):
        probs, invs = [], []
        for p in range(N_KV_HEADS // 2):
            q2 = q_ref[:, g * blk + p * LANES:g * blk + (p + 1) * LANES]
            s = _dot_nt(q2, kbd[p])
            for half in range(2):
                sink = sink_ref[g * N_KV_HEADS + 2 * p + half] * LOG2E
                sw = s[:, half * nkeys:half * nkeys + wlen] + bias
                sc = s[:, half * nkeys + wlen:(half + 1) * nkeys]
                m = jnp.maximum(jnp.maximum(jnp.max(sw, axis=-1, keepdims=True),
                                            jnp.max(sc, axis=-1, keepdims=True)), sink)
                pw, pc = jnp.exp2(sw - m), jnp.exp2(sc - m)
                den = jnp.sum(pw, axis=-1, keepdims=True) + jnp.sum(pc, axis=-1, keepdims=True) + jnp.exp2(sink - m)
                probs += [pw.astype(BF16), pc.astype(BF16)]
                invs.append(1.0 / den)
        o = _dot(jnp.concatenate(probs, axis=1), vbd)
        inv = jnp.where(kv_of_lane == 0, invs[0], jnp.where(kv_of_lane == 1, invs[1],
                                                            jnp.where(kv_of_lane == 2, invs[2], invs[3])))
        o_ref[:, g * blk:(g + 1) * blk] = (o * inv).astype(BF16)
    x = jnp.where(is_ctx, c_ref[...], x_ref[...])
    y_ref[...] = x + mod_ref[...][:, 2 * d:3 * d] * _dot(o_ref[...], w_ref[...])


def _attention(sink, lat, con, mod, q, k, v, w_out, seq, ctx):
    b, t, nq = q.shape
    nk = k.shape[-1]
    d = w_out.shape[1]
    n_lat = seq // QBLK
    off = n_lat if con is lat else 0
    return pl.pallas_call(
        functools.partial(_attn_kernel, seq=seq, ctx=ctx, d=d),
        grid=(b, t // QBLK),
        in_specs=[pl.BlockSpec(memory_space=pltpu.SMEM),
                  pl.BlockSpec((None, QBLK, d), lambda b, n: (b, jnp.minimum(n, n_lat - 1), 0)),
                  pl.BlockSpec((None, QBLK, d), lambda b, n: (b, off + jnp.maximum(n - n_lat, 0), 0)),
                  pl.BlockSpec((None, 1, 6 * d), lambda b, n: (jnp.where(n >= n_lat, mod.shape[0] - 1, b), 0, 0)),
                  pl.BlockSpec((None, QBLK, nq), lambda b, n: (b, n, 0)),
                  pl.BlockSpec((None, t, nk), lambda b, n: (b, 0, 0)),
                  pl.BlockSpec((None, t, nk), lambda b, n: (b, 0, 0)),
                  _resident(w_out.shape)],
        out_specs=pl.BlockSpec((None, QBLK, d), lambda b, n: (b, n, 0)),
        out_shape=jax.ShapeDtypeStruct((b, t, d), F32),
        scratch_shapes=[pltpu.VMEM((QBLK, nq), BF16)],
        compiler_params=_params(("parallel", "parallel")),
        name="attention",
    )(sink, lat, con, mod, q, k, v, w_out)


PROJ_COLS = 256


def _chunk_cumsum(g, reverse):
    n = g.shape[-1]
    groups = CHUNK // SUBLANES
    g = g.reshape(groups, SUBLANES, n)
    r = lax.broadcasted_iota(jnp.int32, g.shape, 1)
    d = 1
    while d < SUBLANES:
        if reverse:
            g = g + jnp.where(r < SUBLANES - d, pltpu.roll(g, SUBLANES - d, axis=1), 0.0)
        else:
            g = g + jnp.where(r >= d, pltpu.roll(g, d, axis=1), 0.0)
        d *= 2
    g = g.reshape(CHUNK, n)
    edge = 0 if reverse else SUBLANES - 1
    parts = [None] * groups
    carry = None
    for j in (reversed(range(groups)) if reverse else range(groups)):
        blk = g[j * SUBLANES:(j + 1) * SUBLANES]
        if carry is not None:
            blk = blk + carry
        parts[j] = blk
        carry = blk[edge:edge + 1]
    return jnp.concatenate(parts, axis=0)


def _hgrn_proj_kernel(x_ref, modl_ref, modc_ref, gain_ref, w_ref, lbl_ref, q_ref, kf_ref, bf_ref, kb_ref, bb_ref,
                      v_ref, gate_ref, acc_ref, *, st, layer):
    d, nb = st.d, st.nb
    h = jnp.concatenate([_norm_mod(x_ref[r], gain_ref[...], m[:, d:2 * d], m[:, 0:d]).astype(BF16)
                         for r, m in enumerate(st.read_mod(modl_ref, modc_ref))], axis=0)
    lg = lbl_ref[...]
    e = jnp.exp(lg - jnp.max(lg, axis=0, keepdims=True))
    lb = jnp.sum(e[1:layer + 1], axis=0, keepdims=True) / jnp.sum(e, axis=0, keepdims=True)
    w = HG_HEADS * HG_DK

    def put(ref, cs, val):
        for r in range(nb):
            ref[r, :, cs] = val[r * TILE:(r + 1) * TILE]

    def finish(seg, c, acc):
        cs = slice(c * PROJ_COLS, (c + 1) * PROJ_COLS)
        if seg == 0:
            put(q_ref, cs, _silu(acc).astype(BF16))
        elif seg in (1, 2):
            k_ref, b_ref = (kf_ref, bf_ref) if seg == 1 else (kb_ref, bb_ref)
            f = lb[:, cs] + (1.0 - lb[:, cs]) * _sigmoid(acc)
            put(k_ref, cs, (1.0 - f).astype(BF16))
            g = jnp.log2(f)
            for r in range(nb):
                for j in range(TILE // CHUNK):
                    rows = slice(r * TILE + j * CHUNK, r * TILE + (j + 1) * CHUNK)
                    b_ref[r, j * CHUNK:(j + 1) * CHUNK, cs] = _chunk_cumsum(g[rows], reverse=seg == 2)
        elif seg == 3:
            put(v_ref, cs, acc.astype(BF16))
        else:
            put(gate_ref, cs, _silu(acc).astype(BF16))

    base = jnp.minimum(pl.program_id(0), 0)
    items = [(seg, c) for c in range(w // PROJ_COLS) for seg in (1, 2, 0, 4, 3)]
    pending = None
    for i, (seg, c) in enumerate(items):
        acc_ref[base + i % 2] = _dot(h, w_ref[:, seg * w + c * PROJ_COLS:seg * w + (c + 1) * PROJ_COLS])
        if pending is not None:
            finish(*pending[:2], acc_ref[base + pending[2]])
        pending = (seg, c, i % 2)
    finish(*pending[:2], acc_ref[base + pending[2]])


def _hgrn_proj(st, xs, mod, gain, w, lb_logits, layer):
    b, t, d = xs.shape
    wd = HG_HEADS * HG_DK
    assert wd == d
    shapes = [(wd, BF16), (wd, BF16), (wd, F32), (wd, BF16), (wd, F32), (d, BF16), (d, BF16)]
    return pl.pallas_call(
        functools.partial(_hgrn_proj_kernel, st=st, layer=layer),
        grid=st.grid,
        in_specs=[st.tile(d), *st.mod(), st.row(d), _resident(w.shape), _resident(lb_logits.shape)],
        out_specs=[st.tile(n) for n, _ in shapes],
        out_shape=[jax.ShapeDtypeStruct((b, t, n), dt) for n, dt in shapes],
        scratch_shapes=[pltpu.VMEM((2, st.nb * TILE, PROJ_COLS), F32)],
        compiler_params=_params(("parallel", "parallel")),
        name="hgrn_proj",
    )(xs, mod, mod, gain, w, lb_logits)


def _level_table(reverse):
    t = np.arange(CHUNK)[:, None]
    s = np.arange(CHUNK)[None, :]
    x = t ^ s
    lvl = np.floor(np.log2(np.maximum(x, 1))).astype(np.int32)
    later = (t < s) if reverse else (t > s)
    lvl = np.where(later, lvl, -1)
    for i in range(CHUNK):
        j = i if reverse else ~i & (CHUNK - 1)
        lvl[i, i] = N_LEVELS if j == 0 else (j & -j).bit_length() - 1
    return np.concatenate([lvl, lvl], axis=1).astype(np.float32)


SLAB = 16


def _neg_abs(x):
    sign = jnp.int32(-2 ** 31)
    return lax.bitcast_convert_type(lax.bitcast_convert_type(x, jnp.int32) | sign, F32)


def _mid_rows(b2, half, reverse):
    n = b2.shape[-1]
    blk = 2 * half
    m = half if reverse else half - 1
    if half == 1:
        b3 = b2.reshape(CHUNK // SUBLANES, SUBLANES, n)
        odd = lax.broadcasted_iota(jnp.int32, b3.shape, 1) % 2 == 1
        if reverse:
            return jnp.where(odd, b3, pltpu.roll(b3, SUBLANES - 1, axis=1)).reshape(CHUNK, n)
        return jnp.where(odd, pltpu.roll(b3, 1, axis=1), b3).reshape(CHUNK, n)
    if blk >= SUBLANES:
        b3 = b2.reshape(CHUNK // blk, blk, n)
        return jnp.broadcast_to(b3[:, m:m + 1, :], b3.shape).reshape(CHUNK, n)
    b3 = b2.reshape(CHUNK // SUBLANES, SUBLANES, n)
    r = lax.broadcasted_iota(jnp.int32, b3.shape, 1)
    mid = jnp.broadcast_to(b3[:, m:m + 1, :], b3.shape)
    for j in range(1, SUBLANES // blk):
        mid = jnp.where(r >= j * blk, jnp.broadcast_to(b3[:, j * blk + m:j * blk + m + 1, :], b3.shape), mid)
    return mid.reshape(CHUNK, n)


def _scan_chunk(lvl_ref, q_ref, k_ref, b_ref, v_ref, o_ref, st_ref, reverse):
    @pl.when(pl.program_id(1) == 0)
    def _():
        st_ref[...] = jnp.zeros(st_ref.shape, F32)

    pw = 2 * HG_DK
    lvl = lvl_ref[...]
    lane = lax.broadcasted_iota(jnp.int32, (1, pw), 1)
    same_head = (lax.broadcasted_iota(jnp.int32, (pw, pw), 0) < HG_DK) == (lane < HG_DK)
    zero = jnp.zeros((), BF16)
    edge = 0 if reverse else CHUNK - 1
    slab = slice(0, SLAB) if reverse else slice(CHUNK - SLAB, CHUNK)

    def pair_rows(x):
        z = jnp.zeros((CHUNK, HG_DK), x.dtype)
        return jnp.concatenate([jnp.concatenate([x[:, :HG_DK], z], axis=1),
                                jnp.concatenate([z, x[:, HG_DK:]], axis=1)], axis=0)

    def pair_cols(xt):
        z = jnp.zeros((HG_DK, CHUNK), xt.dtype)
        return jnp.concatenate([jnp.concatenate([xt[:HG_DK], z], axis=1),
                                jnp.concatenate([z, xt[HG_DK:]], axis=1)], axis=0)

    for r, p in [(r, p) for r in range(q_ref.shape[0]) for p in range(HG_HEADS // 2)]:
        cs = slice(p * pw, (p + 1) * pw)
        q2 = q_ref[r, :, cs]
        k2 = k_ref[r, :, cs]
        b2 = b_ref[r, :, cs]
        v2 = v_ref[r, :, cs]
        b_edge = b2[edge:edge + 1, :]
        st = st_ref[r, p]
        o = _dot_nt(q2 * jnp.exp2(b2).astype(BF16), st.astype(BF16))
        qk = q2[slab].astype(F32) * k2[slab].astype(F32)
        diag = jnp.where(lane < CHUNK, jnp.sum(qk[:, :HG_DK], axis=-1, keepdims=True),
                         jnp.sum(qk[:, HG_DK:], axis=-1, keepdims=True))
        last = jnp.where(lvl[slab] == N_LEVELS, diag.astype(BF16), zero)
        rest = jnp.zeros((CHUNK - SLAB, pw), BF16)
        a = jnp.concatenate([last, rest] if reverse else [rest, last], axis=0)
        kt = k2.T
        for l in range(N_LEVELS):
            e = jnp.exp2(_neg_abs(b2 - _mid_rows(b2, 2 ** l, reverse))).astype(BF16)
            a_l = _dot(q2 * e, pair_cols(kt * e.T))
            a = jnp.where(lvl == l, a_l.astype(BF16), a)
        o = o + _dot(a, pair_rows(v2))
        o_ref[r, :, cs] = o.astype(o_ref.dtype)
        kdec = k2 * jnp.exp2(b_edge - b2).astype(BF16)
        upd = _dot_tn(v2, kdec)
        st_ref[r, p] = st * jnp.exp2(b_edge) + jnp.where(same_head, upd, 0.0)


def _scan_fwd_kernel(lvl_ref, q_ref, k_ref, b_ref, v_ref, o_ref, st_ref):
    _scan_chunk(lvl_ref, q_ref, k_ref, b_ref, v_ref, o_ref, st_ref, reverse=False)


def _scan_bwd_out_kernel(lvl_ref, q_ref, k_ref, b_ref, v_ref, of_ref, gate_ref, x_ref, modl_ref, modc_ref, og_ref,
                         w_ref, y_ref, st_ref, ob_ref, *, n_ctx, d):
    _scan_chunk(lvl_ref, q_ref, k_ref, b_ref, v_ref, ob_ref, st_ref, reverse=True)
    rows = q_ref.shape[0]
    og = og_ref[...]
    parts = []
    for r in range(rows):
        o = of_ref[r].astype(F32) + ob_ref[r]
        heads = []
        for hd in range(HG_HEADS):
            oh = o[:, hd * HG_DK:(hd + 1) * HG_DK]
            ms = jnp.mean(oh * oh, axis=-1, keepdims=True)
            heads.append(oh * lax.rsqrt(ms + EPS) * og)
        parts.append((jnp.concatenate(heads, axis=1) * gate_ref[r].astype(F32)).astype(BF16))
    acc = _dot(jnp.concatenate(parts, axis=0), w_ref[...])
    is_ctx = pl.program_id(1) < n_ctx
    for r in range(rows):
        g1 = jnp.where(is_ctx, modc_ref[...], modl_ref[r])[:, 2 * d:3 * d]
        y_ref[r] = x_ref[r] + g1 * acc[r * CHUNK:(r + 1) * CHUNK]


SCAN_ROWS = 2


def _scan_specs(b, w, seq, ctx, reverse):
    n_lat, n_ctx = seq // CHUNK, ctx // CHUNK
    n = n_lat + n_ctx

    def chunk(i):
        if reverse:
            return n - 1 - i
        return jnp.where(i < n_ctx, n_lat + i, i - n_ctx)

    rows = SCAN_ROWS if b % SCAN_ROWS == 0 else 1
    return rows, n, pl.BlockSpec((rows, CHUNK, w), lambda b, i: (b, chunk(i), 0))


def _hgrn_scan_fwd(lvl, q, k, bcum, v, seq, ctx):
    b, t, w = q.shape
    rows, n, spec = _scan_specs(b, w, seq, ctx, reverse=False)
    return pl.pallas_call(
        _scan_fwd_kernel,
        grid=(b // rows, n),
        in_specs=[_resident(lvl.shape), spec, spec, spec, spec],
        out_specs=spec,
        out_shape=jax.ShapeDtypeStruct((b, t, w), BF16),
        scratch_shapes=[pltpu.VMEM((rows, HG_HEADS // 2, 2 * HG_DK, 2 * HG_DK), F32)],
        compiler_params=_params(("parallel", "arbitrary")),
        name="hgrn_scan_fwd",
    )(lvl, q, k, bcum, v)


def _hgrn_scan_bwd_out(lvl, q, k, bcum, v, o_f, gate, xs, mod, o_gain, w_out, seq, ctx):
    b, t, w = q.shape
    d = w_out.shape[1]
    assert w == d
    rows, n, spec = _scan_specs(b, w, seq, ctx, reverse=True)
    return pl.pallas_call(
        functools.partial(_scan_bwd_out_kernel, n_ctx=ctx // CHUNK, d=d),
        grid=(b // rows, n),
        in_specs=[_resident(lvl.shape), spec, spec, spec, spec, spec, spec, spec,
                  pl.BlockSpec((rows, 1, 6 * d), lambda b, i: (b, 0, 0)),
                  pl.BlockSpec((None, 1, 6 * d), lambda b, i: (mod.shape[0] - 1, 0, 0)),
                  pl.BlockSpec((1, HG_DK), lambda b, i: (0, 0)), _resident(w_out.shape)],
        out_specs=spec,
        out_shape=jax.ShapeDtypeStruct((b, t, d), F32),
        scratch_shapes=[pltpu.VMEM((rows, HG_HEADS // 2, 2 * HG_DK, 2 * HG_DK), F32),
                        pltpu.VMEM((rows, CHUNK, w), F32)],
        compiler_params=_params(("parallel", "arbitrary")),
        name="hgrn_scan_bwd_out",
    )(lvl, q, k, bcum, v, o_f, gate, xs, mod, mod, o_gain, w_out)


FFN_COLS = 256


def _ffn_kernel(x_ref, prev_ref, next_ref, modl_ref, modc_ref, gain_ref, wu_ref, cw_ref, cb_ref, wd_ref, y_ref, u_ref,
                *, st):
    d, nb, n_lat, n_ctx = st.d, st.nb, st.n_lat, st.n_ctx
    t = pl.program_id(1)
    has_prev = jnp.logical_and(t != 0, t != n_lat).astype(F32)
    has_next = jnp.logical_and(t != n_lat - 1, t != n_lat + n_ctx - 1).astype(F32)
    keep = jnp.where(lax.broadcasted_iota(jnp.int32, (2 * SUBLANES, 1), 0) < SUBLANES, has_prev, has_next)
    gain = gain_ref[...]
    mods = st.read_mod(modl_ref, modc_ref)
    xs = [x_ref[r] for r in range(nb)]
    h = jnp.concatenate([_norm_mod(x, gain, m[:, 4 * d:5 * d], m[:, 3 * d:4 * d]).astype(BF16)
                         for x, m in zip(xs, mods)], axis=0)
    halos = [(_norm_mod(jnp.concatenate([prev_ref[r], next_ref[r]], axis=0), gain, m[:, 4 * d:5 * d],
                        m[:, 3 * d:4 * d]) * keep).astype(BF16) for r, m in enumerate(mods)]
    h_ext = jnp.concatenate([h] + halos, axis=0)
    f = wd_ref.shape[0]
    row = lax.broadcasted_iota(jnp.int32, (TILE, 1), 0)
    for c in range(f // FFN_COLS):
        cs = slice(c * FFN_COLS, (c + 1) * FFN_COLS)
        ge = _dot(h_ext, wu_ref[:, cs])
        val = _dot(h, wu_ref[:, f + c * FFN_COLS:f + (c + 1) * FFN_COLS])
        for r in range(nb):
            g = ge[r * TILE:(r + 1) * TILE]
            hb = nb * TILE + 2 * SUBLANES * r
            before = ge[hb + SUBLANES - 1:hb + SUBLANES]
            after = ge[hb + SUBLANES:hb + SUBLANES + 1]
            up = jnp.where(row == 0, before, pltpu.roll(g, 1, axis=0))
            dn = jnp.where(row == TILE - 1, after, pltpu.roll(g, TILE - 1, axis=0))
            conv = up * cw_ref[0:1, cs] + g * cw_ref[1:2, cs] + dn * cw_ref[2:3, cs] + cb_ref[:, cs]
            u_ref[r * TILE:(r + 1) * TILE, cs] = (_silu(conv) * val[r * TILE:(r + 1) * TILE]).astype(BF16)
    acc = _dot(u_ref[...], wd_ref[...])
    for r in range(nb):
        y_ref[r] = xs[r] + mods[r][:, 5 * d:6 * d] * acc[r * TILE:(r + 1) * TILE]


def _ffn(st, xs, mod, gain, w_up, conv_w, conv_b, w_down):
    d = st.d
    f = w_down.shape[0]
    per_tile = TILE // SUBLANES
    last = st.rows // SUBLANES - 1
    return pl.pallas_call(
        functools.partial(_ffn_kernel, st=st),
        grid=st.grid,
        in_specs=[st.tile(d),
                  pl.BlockSpec((st.nb, SUBLANES, d), lambda b, t: (b, jnp.maximum(t * per_tile - 1, 0), 0)),
                  pl.BlockSpec((st.nb, SUBLANES, d), lambda b, t: (b, jnp.minimum((t + 1) * per_tile, last), 0)),
                  *st.mod(), st.row(d), _resident(w_up.shape), _resident(conv_w.shape), _resident(conv_b.shape),
                  _resident(w_down.shape)],
        out_specs=st.tile(d),
        out_shape=jax.ShapeDtypeStruct((st.batch, st.rows, d), F32),
        scratch_shapes=[pltpu.VMEM((st.nb * TILE, f), BF16)],
        compiler_params=_params(("parallel", "parallel")),
        name="ffn",
    )(xs, xs, xs, mod, mod, gain, w_up, conv_w, conv_b, w_down)


def _rope_tables(seq, ctx):
    rows = seq // GRID_W
    row = jnp.repeat(jnp.arange(rows, dtype=F32), GRID_W)
    col = jnp.tile(jnp.arange(GRID_W, dtype=F32), rows)
    n_pairs = HEAD_DIM // 4
    inv = ROPE_THETA ** (-jnp.arange(n_pairs, dtype=F32) / n_pairs)
    ang_r, ang_c = row[:, None] * inv, col[:, None] * inv
    cos = jnp.concatenate([jnp.cos(ang_r)] * 2 + [jnp.cos(ang_c)] * 2, axis=1)
    sin = jnp.concatenate([-jnp.sin(ang_r), jnp.sin(ang_r), -jnp.sin(ang_c), jnp.sin(ang_c)], axis=1)
    cos = jnp.concatenate([cos, jnp.ones((ctx, HEAD_DIM), F32)], axis=0)
    sin = jnp.concatenate([sin, jnp.zeros((ctx, HEAD_DIM), F32)], axis=0)
    return jnp.tile(cos, (1, 4)), jnp.tile(sin, (1, 4))


def _head_block_constants():
    blk = 4 * HEAD_DIM
    i = np.arange(blk)
    gm = (i[:, None] // HEAD_DIM == i[None, :] // HEAD_DIM).astype(np.float32) / HEAD_DIM
    quarter = HEAD_DIM // 4
    partner = np.where((i % (2 * quarter)) < quarter, i + quarter, i - quarter)
    rot = (i[:, None] == partner[None, :]).astype(np.float32)
    return jnp.asarray(gm, BF16), jnp.asarray(rot, BF16)


def _query_head_order():
    return np.array([kh * GROUP + g for g in range(GROUP) for kh in range(N_KV_HEADS)])


def kernel(x, c, ctx, c_ctx, ada_w, ada_b, norm1_g, norm2_g, attn_w_in, attn_w_out, attn_q_gain, attn_k_gain,
           attn_sink, hgrn_w_in, hgrn_w_out, hgrn_o_gain, hgrn_lb_logits, ffn_w_up, ffn_conv_w, ffn_conv_b,
           ffn_w_down):
    batch, seq, d = x.shape
    n_ctx_tok = ctx.shape[1]
    depth = ada_w.shape[0]
    assert seq % TILE == 0 and n_ctx_tok % TILE == 0 and seq >= QBLK + 2 * WINDOW
    st = _Stream(batch, seq // TILE, n_ctx_tok // TILE, d)

    rows = -(-(batch + 1) // SUBLANES) * SUBLANES
    cc = jnp.zeros((rows, d), F32).at[:batch].set(c).at[batch].set(c_ctx)
    mod_all = _modulation(cc, ada_w, ada_b)[:, :batch + 1].reshape(depth, batch + 1, 1, 6 * d)

    cos, sin = _rope_tables(seq, n_ctx_tok)
    gm, rot = _head_block_constants()
    order = _query_head_order()
    nq = N_HEADS * HEAD_DIM
    lvl_f, lvl_b = jnp.asarray(_level_table(False), BF16), jnp.asarray(_level_table(True), BF16)

    st_lat = _Stream(batch, seq // TILE, 0, d)
    lat, con = x, ctx
    for layer in range(depth):
        j = layer // 2
        mod = mod_all[layer]
        g1 = norm1_g[layer].reshape(1, d)
        st_out = st_lat if layer == depth - 1 else st
        if layer % 2 == 0:
            w_in = attn_w_in[j]
            wq = w_in[:, :nq].reshape(d, N_HEADS, HEAD_DIM)[:, order].reshape(d, nq)
            w_in = jnp.concatenate([wq, w_in[:, nq:]], axis=1).astype(BF16)
            w_out = attn_w_out[j].reshape(N_HEADS, HEAD_DIM, d)[order].reshape(nq, d).astype(BF16)
            qg = jnp.tile(attn_q_gain[j], 4).reshape(1, 4 * HEAD_DIM)
            kg = jnp.tile(attn_k_gain[j], 4).reshape(1, 4 * HEAD_DIM)
            q, k, v = _attn_proj(st, lat, con, mod, g1, w_in, qg, kg, cos, sin, gm, rot)
            xs = _attention(attn_sink[j][order], lat, con, mod, q, k, v, w_out, seq, n_ctx_tok)
        else:
            q, kf, bf, kb, bb, v, gate = _hgrn_proj(st, lat, mod, g1, hgrn_w_in[j].astype(BF16), hgrn_lb_logits,
                                                    layer)
            o_f = _hgrn_scan_fwd(lvl_f, q, kf, bf, v, seq, n_ctx_tok)
            xs = _hgrn_scan_bwd_out(lvl_b, q, kb, bb, v, o_f, gate, lat, mod, hgrn_o_gain[j].reshape(1, HG_DK),
                                    hgrn_w_out[j].astype(BF16), seq, n_ctx_tok)
        xs = _ffn(st_out, xs, mod, norm2_g[layer].reshape(1, d), ffn_w_up[layer].astype(BF16), ffn_conv_w[layer],
                  ffn_conv_b[layer].reshape(1, -1), ffn_w_down[layer].astype(BF16))
        lat = con = xs
    return xs
```

```python
import functools
import math

import jax
import jax.numpy as jnp
import numpy as np
from jax import lax
from jax.experimental import pallas as pl
from jax.experimental.pallas import tpu as pltpu

F32 = jnp.float32
BF16 = jnp.bfloat16

EPS = 1e-6
HEAD_DIM = 64
N_HEADS = 16
N_KV_HEADS = 4
GROUP = N_HEADS // N_KV_HEADS
WINDOW = 128
ROPE_THETA = 10000.0
GRID_W = 64
HG_HEADS = 8
HG_DK = 128
CHUNK = 128
N_LEVELS = 7
TILE = 256
STREAM_ROWS = 2
QBLK = 128
ATTN_ROWS = 2
LANES = 128
SUBLANES = 8
NEG_BIG = -1e30
LOG2E = math.log2(math.e)
VMEM_LIMIT = 56 * 1024 * 1024


def _silu(x):
    return x / (1.0 + jnp.exp(-x))


def _sigmoid(x):
    return 1.0 / (1.0 + jnp.exp(-x))


def _dot(a, b):
    return jnp.dot(a, b, preferred_element_type=F32)


def _dot_nt(a, b):
    return lax.dot_general(a, b, (((1,), (1,)), ((), ())), preferred_element_type=F32)


def _dot_tn(a, b):
    return lax.dot_general(a, b, (((0,), (0,)), ((), ())), preferred_element_type=F32)


def _norm_mod(x, gain, scale, shift):
    ms = jnp.mean(x * x, axis=-1, keepdims=True)
    return (x * lax.rsqrt(ms + EPS)) * gain * (1.0 + scale) + shift


def _params(sem):
    return pltpu.CompilerParams(dimension_semantics=sem, vmem_limit_bytes=VMEM_LIMIT)


def _resident(shape):
    nd = len(shape)
    return pl.BlockSpec(shape, lambda *_: (0,) * nd, pipeline_mode=pl.Buffered(1))


def _mod_kernel(cc_ref, w_ref, b_ref, o_ref):
    a = _silu(cc_ref[...]).astype(BF16)
    o_ref[...] = _dot(a, w_ref[...].astype(BF16)) + b_ref[...]


def _modulation(cc, ada_w, ada_b):
    depth, d, n = ada_w.shape
    r = cc.shape[0]
    tn = 1536
    return pl.pallas_call(
        _mod_kernel,
        grid=(depth, n // tn),
        in_specs=[
            pl.BlockSpec((r, d), lambda l, j: (0, 0)),
            pl.BlockSpec((None, d, tn), lambda l, j: (l, 0, j)),
            pl.BlockSpec((None, 1, tn), lambda l, j: (l, 0, j)),
        ],
        out_specs=pl.BlockSpec((None, r, tn), lambda l, j: (l, 0, j)),
        out_shape=jax.ShapeDtypeStruct((depth, r, n), F32),
        compiler_params=_params(("parallel", "parallel")),
        name="modulation",
    )(cc, ada_w, ada_b.reshape(depth, 1, n))


class _Stream:
    def __init__(self, batch, n_lat, n_ctx, d):
        self.batch, self.n_lat, self.n_ctx, self.d = batch, n_lat, n_ctx, d
        self.nb = STREAM_ROWS if batch % STREAM_ROWS == 0 else 1
        self.grid = (batch // self.nb, n_lat + n_ctx)
        self.rows = (n_lat + n_ctx) * TILE

    def tile(self, width, col=0):
        return pl.BlockSpec((self.nb, TILE, width), lambda b, t: (b, t, col))

    def sources(self, lat, ctx):
        n_lat = self.n_lat
        off = n_lat if ctx is lat else 0
        return [pl.BlockSpec((self.nb, TILE, self.d), lambda b, t: (b, jnp.minimum(t, n_lat - 1), 0)),
                pl.BlockSpec((self.nb, TILE, self.d), lambda b, t: (b, off + jnp.maximum(t - n_lat, 0), 0))]

    def is_ctx(self):
        return pl.program_id(1) >= self.n_lat

    def read(self, lat_ref, ctx_ref):
        return [jnp.where(self.is_ctx(), ctx_ref[r], lat_ref[r]) for r in range(self.nb)]

    def mod(self):
        return [pl.BlockSpec((self.nb, 1, 6 * self.d), lambda b, t: (b, 0, 0)),
                pl.BlockSpec((None, 1, 6 * self.d), lambda b, t: (self.batch, 0, 0))]

    def read_mod(self, lat_ref, ctx_ref):
        return [jnp.where(self.is_ctx(), ctx_ref[...], lat_ref[r]) for r in range(self.nb)]

    def row(self, width):
        return pl.BlockSpec((1, width), lambda b, t: (0, 0))


def _attn_proj_kernel(x_ref, c_ref, modl_ref, modc_ref, gain_ref, w_ref, qg_ref, kg_ref, cos_ref, sin_ref, gm_ref,
                      rot_ref, q_ref, k_ref, v_ref, *, st):
    d, nb = st.d, st.nb
    mods = st.read_mod(modl_ref, modc_ref)
    h = jnp.concatenate([_norm_mod(x, gain_ref[...], m[:, d:2 * d], m[:, 0:d]).astype(BF16)
                         for x, m in zip(st.read(x_ref, c_ref), mods)], axis=0)
    acc = _dot(h, w_ref[...])
    cos = jnp.concatenate([cos_ref[...]] * nb, axis=0)
    sin = jnp.concatenate([sin_ref[...]] * nb, axis=0)
    gm, rot = gm_ref[...], rot_ref[...]
    blk = 4 * HEAD_DIM
    n_qblk = N_HEADS * HEAD_DIM // blk

    def put(ref, cs, val):
        for r in range(nb):
            ref[r, :, cs] = val[r * TILE:(r + 1) * TILE]

    for c in range(n_qblk + 1):
        xc = acc[:, c * blk:(c + 1) * blk]
        ms = _dot((xc * xc).astype(BF16), gm)
        gain = qg_ref[...] if c < n_qblk else kg_ref[...]
        xn = xc * lax.rsqrt(ms + EPS) * gain
        y = xn * cos + _dot(xn.astype(BF16), rot) * sin
        if c < n_qblk:
            put(q_ref, slice(c * blk, (c + 1) * blk), (y * (HEAD_DIM ** -0.5 * LOG2E)).astype(BF16))
        else:
            put(k_ref, slice(None), y.astype(BF16))
    put(v_ref, slice(None), acc[:, (n_qblk + 1) * blk:].astype(BF16))


def _attn_proj(st, lat, ctx, mod, gain, w, qg, kg, cos, sin, gm, rot):
    b, d = st.batch, st.d
    t = st.rows
    nq, nk = N_HEADS * HEAD_DIM, N_KV_HEADS * HEAD_DIM
    return pl.pallas_call(
        functools.partial(_attn_proj_kernel, st=st),
        grid=st.grid,
        in_specs=[*st.sources(lat, ctx), *st.mod(), st.row(d), _resident(w.shape), st.row(nk), st.row(nk),
                  pl.BlockSpec((TILE, nk), lambda b, t: (t, 0)), pl.BlockSpec((TILE, nk), lambda b, t: (t, 0)),
                  _resident(gm.shape), _resident(rot.shape)],
        out_specs=[st.tile(nq), st.tile(nk), st.tile(nk)],
        out_shape=[jax.ShapeDtypeStruct((b, t, nq), BF16), jax.ShapeDtypeStruct((b, t, nk), BF16),
                   jax.ShapeDtypeStruct((b, t, nk), BF16)],
        compiler_params=_params(("parallel", "parallel")),
        name="attn_proj",
    )(lat, ctx, mod, mod, gain, w, qg, kg, cos, sin, gm, rot)


def _attn_kernel(sink_ref, x_ref, c_ref, modl_ref, modc_ref, q_ref, k_ref, v_ref, w_ref, y_ref, o_ref,
                 *, seq, ctx, d):
    n = pl.program_id(1)
    n_lat = seq // QBLK
    wlen = QBLK + 2 * WINDOW
    is_ctx = n >= n_lat
    ws = pl.multiple_of(jnp.clip(n * QBLK - WINDOW, 0, seq - wlen), QBLK)
    nkeys = wlen + ctx
    qpos = n * QBLK + lax.broadcasted_iota(jnp.int32, (QBLK, wlen), 0)
    kpos = ws + lax.broadcasted_iota(jnp.int32, (QBLK, wlen), 1)
    in_window = (jnp.abs(qpos - kpos) <= WINDOW) & jnp.logical_not(is_ctx)
    bias = jnp.where(in_window, 0.0, NEG_BIG).astype(F32)
    left = lax.broadcasted_iota(jnp.int32, (1, LANES), 1) < HEAD_DIM
    kv_of_lane = lax.broadcasted_iota(jnp.int32, (1, N_KV_HEADS * HEAD_DIM), 1) // HEAD_DIM
    zero = jnp.zeros((), BF16)
    blk = N_KV_HEADS * HEAD_DIM
    nb = q_ref.shape[0]
    for r, g in [(r, g) for r in range(nb) for g in range(GROUP)]:
        if g == 0:
            kall = jnp.concatenate([k_ref[r, pl.ds(ws, wlen), :], k_ref[r, seq:seq + ctx, :]], axis=0)
            vall = jnp.concatenate([v_ref[r, pl.ds(ws, wlen), :], v_ref[r, seq:seq + ctx, :]], axis=0)
            kbd = [jnp.concatenate([jnp.where(left, kall[:, p * LANES:(p + 1) * LANES], zero),
                                    jnp.where(left, zero, kall[:, p * LANES:(p + 1) * LANES])], axis=0)
                   for p in range(N_KV_HEADS // 2)]
            vbd = jnp.concatenate([jnp.where(kv_of_lane == kh, vall, zero) for kh in range(N_KV_HEADS)], axis=0)
        probs, invs = [], []
        for p in range(N_KV_HEADS // 2):
            q2 = q_ref[r, :, g * blk + p * LANES:g * blk + (p + 1) * LANES]
            s = _dot_nt(q2, kbd[p])
            for half in range(2):
                sink = sink_ref[g * N_KV_HEADS + 2 * p + half] * LOG2E
                sw = s[:, half * nkeys:half * nkeys + wlen] + bias
                sc = s[:, half * nkeys + wlen:(half + 1) * nkeys]
                m = jnp.maximum(jnp.maximum(jnp.max(sw, axis=-1, keepdims=True),
                                            jnp.max(sc, axis=-1, keepdims=True)), sink)
                pw, pc = jnp.exp2(sw - m), jnp.exp2(sc - m)
                den = jnp.sum(pw, axis=-1, keepdims=True) + jnp.sum(pc, axis=-1, keepdims=True) + jnp.exp2(sink - m)
                probs += [pw.astype(BF16), pc.astype(BF16)]
                invs.append(1.0 / den)
        o = _dot(jnp.concatenate(probs, axis=1), vbd)
        inv = jnp.where(kv_of_lane == 0, invs[0], jnp.where(kv_of_lane == 1, invs[1],
                                                            jnp.where(kv_of_lane == 2, invs[2], invs[3])))
        o_ref[r * QBLK:(r + 1) * QBLK, g * blk:(g + 1) * blk] = (o * inv).astype(BF16)
    acc = _dot(o_ref[...], w_ref[...])
    for r in range(nb):
        x = jnp.where(is_ctx, c_ref[r], x_ref[r])
        g1 = jnp.where(is_ctx, modc_ref[...], modl_ref[r])[:, 2 * d:3 * d]
        y_ref[r] = x + g1 * acc[r * QBLK:(r + 1) * QBLK]


def _attention(sink, lat, con, mod, q, k, v, w_out, seq, ctx):
    b, t, nq = q.shape
    nk = k.shape[-1]
    d = w_out.shape[1]
    n_lat = seq // QBLK
    off = n_lat if con is lat else 0
    nb = ATTN_ROWS if b % ATTN_ROWS == 0 else 1
    return pl.pallas_call(
        functools.partial(_attn_kernel, seq=seq, ctx=ctx, d=d),
        grid=(b // nb, t // QBLK),
        in_specs=[pl.BlockSpec(memory_space=pltpu.SMEM),
                  pl.BlockSpec((nb, QBLK, d), lambda b, n: (b, jnp.minimum(n, n_lat - 1), 0)),
                  pl.BlockSpec((nb, QBLK, d), lambda b, n: (b, off + jnp.maximum(n - n_lat, 0), 0)),
                  pl.BlockSpec((nb, 1, 6 * d), lambda b, n: (b, 0, 0)),
                  pl.BlockSpec((None, 1, 6 * d), lambda b, n: (mod.shape[0] - 1, 0, 0)),
                  pl.BlockSpec((nb, QBLK, nq), lambda b, n: (b, n, 0)),
                  pl.BlockSpec((nb, t, nk), lambda b, n: (b, 0, 0)),
                  pl.BlockSpec((nb, t, nk), lambda b, n: (b, 0, 0)),
                  _resident(w_out.shape)],
        out_specs=pl.BlockSpec((nb, QBLK, d), lambda b, n: (b, n, 0)),
        out_shape=jax.ShapeDtypeStruct((b, t, d), F32),
        scratch_shapes=[pltpu.VMEM((nb * QBLK, nq), BF16)],
        compiler_params=_params(("parallel", "parallel")),
        name="attention",
    )(sink, lat, con, mod, mod, q, k, v, w_out)


PROJ_COLS = 256


def _chunk_cumsum(g, reverse):
    n = g.shape[-1]
    groups = CHUNK // SUBLANES
    g = g.reshape(groups, SUBLANES, n)
    r = lax.broadcasted_iota(jnp.int32, g.shape, 1)
    d = 1
    while d < SUBLANES:
        if reverse:
            g = g + jnp.where(r < SUBLANES - d, pltpu.roll(g, SUBLANES - d, axis=1), 0.0)
        else:
            g = g + jnp.where(r >= d, pltpu.roll(g, d, axis=1), 0.0)
        d *= 2
    g = g.reshape(CHUNK, n)
    edge = 0 if reverse else SUBLANES - 1
    parts = [None] * groups
    carry = None
    for j in (reversed(range(groups)) if reverse else range(groups)):
        blk = g[j * SUBLANES:(j + 1) * SUBLANES]
        if carry is not None:
            blk = blk + carry
        parts[j] = blk
        carry = blk[edge:edge + 1]
    return jnp.concatenate(parts, axis=0)


def _hgrn_proj_kernel(x_ref, modl_ref, modc_ref, gain_ref, w_ref, lbl_ref, q_ref, kf_ref, bf_ref, kb_ref, bb_ref,
                      v_ref, gate_ref, acc_ref, *, st, layer):
    d, nb = st.d, st.nb
    h = jnp.concatenate([_norm_mod(x_ref[r], gain_ref[...], m[:, d:2 * d], m[:, 0:d]).astype(BF16)
                         for r, m in enumerate(st.read_mod(modl_ref, modc_ref))], axis=0)
    lg = lbl_ref[...]
    e = jnp.exp(lg - jnp.max(lg, axis=0, keepdims=True))
    lb = jnp.sum(e[1:layer + 1], axis=0, keepdims=True) / jnp.sum(e, axis=0, keepdims=True)
    w = HG_HEADS * HG_DK

    def put(ref, cs, val):
        for r in range(nb):
            ref[r, :, cs] = val[r * TILE:(r + 1) * TILE]

    def finish(seg, c, acc):
        cs = slice(c * PROJ_COLS, (c + 1) * PROJ_COLS)
        if seg == 0:
            put(q_ref, cs, _silu(acc).astype(BF16))
        elif seg in (1, 2):
            k_ref, b_ref = (kf_ref, bf_ref) if seg == 1 else (kb_ref, bb_ref)
            f = lb[:, cs] + (1.0 - lb[:, cs]) * _sigmoid(acc)
            put(k_ref, cs, (1.0 - f).astype(BF16))
            g = jnp.log2(f)
            for r in range(nb):
                for j in range(TILE // CHUNK):
                    rows = slice(r * TILE + j * CHUNK, r * TILE + (j + 1) * CHUNK)
                    b_ref[r, j * CHUNK:(j + 1) * CHUNK, cs] = _chunk_cumsum(g[rows], reverse=seg == 2)
        elif seg == 3:
            put(v_ref, cs, acc.astype(BF16))
        else:
            put(gate_ref, cs, _silu(acc).astype(BF16))

    base = jnp.minimum(pl.program_id(0), 0)
    items = [(seg, c) for c in range(w // PROJ_COLS) for seg in (1, 2, 0, 4, 3)]
    pending = None
    for i, (seg, c) in enumerate(items):
        acc_ref[base + i % 2] = _dot(h, w_ref[:, seg * w + c * PROJ_COLS:seg * w + (c + 1) * PROJ_COLS])
        if pending is not None:
            finish(*pending[:2], acc_ref[base + pending[2]])
        pending = (seg, c, i % 2)
    finish(*pending[:2], acc_ref[base + pending[2]])


def _hgrn_proj(st, xs, mod, gain, w, lb_logits, layer):
    b, t, d = xs.shape
    wd = HG_HEADS * HG_DK
    assert wd == d
    shapes = [(wd, BF16), (wd, BF16), (wd, F32), (wd, BF16), (wd, F32), (d, BF16), (d, BF16)]
    return pl.pallas_call(
        functools.partial(_hgrn_proj_kernel, st=st, layer=layer),
        grid=st.grid,
        in_specs=[st.tile(d), *st.mod(), st.row(d), _resident(w.shape), _resident(lb_logits.shape)],
        out_specs=[st.tile(n) for n, _ in shapes],
        out_shape=[jax.ShapeDtypeStruct((b, t, n), dt) for n, dt in shapes],
        scratch_shapes=[pltpu.VMEM((2, st.nb * TILE, PROJ_COLS), F32)],
        compiler_params=_params(("parallel", "parallel")),
        name="hgrn_proj",
    )(xs, mod, mod, gain, w, lb_logits)


def _level_table(reverse):
    t = np.arange(CHUNK)[:, None]
    s = np.arange(CHUNK)[None, :]
    x = t ^ s
    lvl = np.floor(np.log2(np.maximum(x, 1))).astype(np.int32)
    later = (t < s) if reverse else (t > s)
    lvl = np.where(later, lvl, -1)
    for i in range(CHUNK):
        j = i if reverse else ~i & (CHUNK - 1)
        lvl[i, i] = N_LEVELS if j == 0 else (j & -j).bit_length() - 1
    return np.concatenate([lvl, lvl], axis=1).astype(np.float32)


SLAB = 16


def _neg_abs16(x):
    sign = jnp.int16(-2 ** 15)
    return lax.bitcast_convert_type(lax.bitcast_convert_type(x, jnp.int16) | sign, BF16)


def _mid_rows(b2, half, reverse):
    n = b2.shape[-1]
    blk = 2 * half
    m = half if reverse else half - 1
    if half == 1:
        b3 = b2.reshape(CHUNK // SUBLANES, SUBLANES, n)
        odd = lax.broadcasted_iota(jnp.int32, b3.shape, 1) % 2 == 1
        if reverse:
            return jnp.where(odd, b3, pltpu.roll(b3, SUBLANES - 1, axis=1)).reshape(CHUNK, n)
        return jnp.where(odd, pltpu.roll(b3, 1, axis=1), b3).reshape(CHUNK, n)
    if blk >= SUBLANES:
        b3 = b2.reshape(CHUNK // blk, blk, n)
        return jnp.broadcast_to(b3[:, m:m + 1, :], b3.shape).reshape(CHUNK, n)
    b3 = b2.reshape(CHUNK // SUBLANES, SUBLANES, n)
    r = lax.broadcasted_iota(jnp.int32, b3.shape, 1)
    mid = jnp.broadcast_to(b3[:, m:m + 1, :], b3.shape)
    for j in range(1, SUBLANES // blk):
        mid = jnp.where(r >= j * blk, jnp.broadcast_to(b3[:, j * blk + m:j * blk + m + 1, :], b3.shape), mid)
    return mid.reshape(CHUNK, n)


def _scan_chunk(lvl_ref, q_ref, k_ref, b_ref, v_ref, o_ref, st_ref, reverse):
    @pl.when(pl.program_id(1) == 0)
    def _():
        st_ref[...] = jnp.zeros(st_ref.shape, F32)

    pw = 2 * HG_DK
    lvl = lvl_ref[...]
    lane = lax.broadcasted_iota(jnp.int32, (1, pw), 1)
    same_head = (lax.broadcasted_iota(jnp.int32, (pw, pw), 0) < HG_DK) == (lane < HG_DK)
    zero = jnp.zeros((), BF16)
    edge = 0 if reverse else CHUNK - 1
    slab = slice(0, SLAB) if reverse else slice(CHUNK - SLAB, CHUNK)

    def pair_rows(x):
        z = jnp.zeros((CHUNK, HG_DK), x.dtype)
        return jnp.concatenate([jnp.concatenate([x[:, :HG_DK], z], axis=1),
                                jnp.concatenate([z, x[:, HG_DK:]], axis=1)], axis=0)

    def pair_cols(xt):
        z = jnp.zeros((HG_DK, CHUNK), xt.dtype)
        return jnp.concatenate([jnp.concatenate([xt[:HG_DK], z], axis=1),
                                jnp.concatenate([z, xt[HG_DK:]], axis=1)], axis=0)

    for r, p in [(r, p) for r in range(q_ref.shape[0]) for p in range(HG_HEADS // 2)]:
        cs = slice(p * pw, (p + 1) * pw)
        q2 = q_ref[r, :, cs]
        k2 = k_ref[r, :, cs]
        b2 = b_ref[r, :, cs]
        v2 = v_ref[r, :, cs]
        b_edge = b2[edge:edge + 1, :]
        st = st_ref[r, p]
        o = _dot_nt(q2 * jnp.exp2(b2).astype(BF16), st.astype(BF16))
        qk = q2[slab].astype(F32) * k2[slab].astype(F32)
        diag = jnp.where(lane < CHUNK, jnp.sum(qk[:, :HG_DK], axis=-1, keepdims=True),
                         jnp.sum(qk[:, HG_DK:], axis=-1, keepdims=True))
        last = jnp.where(lvl[slab] == N_LEVELS, diag.astype(BF16), zero)
        rest = jnp.zeros((CHUNK - SLAB, pw), BF16)
        a = jnp.concatenate([last, rest] if reverse else [rest, last], axis=0)
        kt = k2.T
        for l in range(N_LEVELS):
            e = jnp.exp2(_neg_abs16((b2 - _mid_rows(b2, 2 ** l, reverse)).astype(BF16)))
            a_l = _dot(q2 * e, pair_cols(kt * e.T))
            a = jnp.where(lvl == l, a_l.astype(BF16), a)
        o = o + _dot(a, pair_rows(v2))
        o_ref[r, :, cs] = o.astype(o_ref.dtype)
        kdec = k2 * jnp.exp2(b_edge - b2).astype(BF16)
        upd = _dot_tn(v2, kdec)
        st_ref[r, p] = st * jnp.exp2(b_edge) + jnp.where(same_head, upd, 0.0)


def _scan_fwd_kernel(lvl_ref, q_ref, k_ref, b_ref, v_ref, o_ref, st_ref):
    _scan_chunk(lvl_ref, q_ref, k_ref, b_ref, v_ref, o_ref, st_ref, reverse=False)


def _scan_bwd_out_kernel(lvl_ref, q_ref, k_ref, b_ref, v_ref, of_ref, gate_ref, x_ref, modl_ref, modc_ref, og_ref,
                         w_ref, y_ref, st_ref, ob_ref, *, n_ctx, d):
    _scan_chunk(lvl_ref, q_ref, k_ref, b_ref, v_ref, ob_ref, st_ref, reverse=True)
    rows = q_ref.shape[0]
    og = og_ref[...]
    parts = []
    for r in range(rows):
        o = of_ref[r].astype(F32) + ob_ref[r]
        heads = []
        for hd in range(HG_HEADS):
            oh = o[:, hd * HG_DK:(hd + 1) * HG_DK]
            ms = jnp.mean(oh * oh, axis=-1, keepdims=True)
            heads.append(oh * lax.rsqrt(ms + EPS) * og)
        parts.append((jnp.concatenate(heads, axis=1) * gate_ref[r].astype(F32)).astype(BF16))
    acc = _dot(jnp.concatenate(parts, axis=0), w_ref[...])
    is_ctx = pl.program_id(1) < n_ctx
    for r in range(rows):
        g1 = jnp.where(is_ctx, modc_ref[...], modl_ref[r])[:, 2 * d:3 * d]
        y_ref[r] = x_ref[r] + g1 * acc[r * CHUNK:(r + 1) * CHUNK]


SCAN_ROWS = 2


def _scan_specs(b, w, seq, ctx, reverse):
    n_lat, n_ctx = seq // CHUNK, ctx // CHUNK
    n = n_lat + n_ctx

    def chunk(i):
        if reverse:
            return n - 1 - i
        return jnp.where(i < n_ctx, n_lat + i, i - n_ctx)

    rows = SCAN_ROWS if b % SCAN_ROWS == 0 else 1
    return rows, n, pl.BlockSpec((rows, CHUNK, w), lambda b, i: (b, chunk(i), 0))


def _hgrn_scan_fwd(lvl, q, k, bcum, v, seq, ctx):
    b, t, w = q.shape
    rows, n, spec = _scan_specs(b, w, seq, ctx, reverse=False)
    return pl.pallas_call(
        _scan_fwd_kernel,
        grid=(b // rows, n),
        in_specs=[_resident(lvl.shape), spec, spec, spec, spec],
        out_specs=spec,
        out_shape=jax.ShapeDtypeStruct((b, t, w), BF16),
        scratch_shapes=[pltpu.VMEM((rows, HG_HEADS // 2, 2 * HG_DK, 2 * HG_DK), F32)],
        compiler_params=_params(("parallel", "arbitrary")),
        name="hgrn_scan_fwd",
    )(lvl, q, k, bcum, v)


def _hgrn_scan_bwd_out(lvl, q, k, bcum, v, o_f, gate, xs, mod, o_gain, w_out, seq, ctx):
    b, t, w = q.shape
    d = w_out.shape[1]
    assert w == d
    rows, n, spec = _scan_specs(b, w, seq, ctx, reverse=True)
    return pl.pallas_call(
        functools.partial(_scan_bwd_out_kernel, n_ctx=ctx // CHUNK, d=d),
        grid=(b // rows, n),
        in_specs=[_resident(lvl.shape), spec, spec, spec, spec, spec, spec, spec,
                  pl.BlockSpec((rows, 1, 6 * d), lambda b, i: (b, 0, 0)),
                  pl.BlockSpec((None, 1, 6 * d), lambda b, i: (mod.shape[0] - 1, 0, 0)),
                  pl.BlockSpec((1, HG_DK), lambda b, i: (0, 0)), _resident(w_out.shape)],
        out_specs=spec,
        out_shape=jax.ShapeDtypeStruct((b, t, d), F32),
        scratch_shapes=[pltpu.VMEM((rows, HG_HEADS // 2, 2 * HG_DK, 2 * HG_DK), F32),
                        pltpu.VMEM((rows, CHUNK, w), F32)],
        compiler_params=_params(("parallel", "arbitrary")),
        name="hgrn_scan_bwd_out",
    )(lvl, q, k, bcum, v, o_f, gate, xs, mod, mod, o_gain, w_out)


FFN_COLS = 256


def _ffn_kernel(x_ref, prev_ref, next_ref, modl_ref, modc_ref, gain_ref, wu_ref, cw_ref, cb_ref, wd_ref, y_ref, u_ref,
                *, st):
    d, nb, n_lat, n_ctx = st.d, st.nb, st.n_lat, st.n_ctx
    t = pl.program_id(1)
    has_prev = jnp.logical_and(t != 0, t != n_lat).astype(F32)
    has_next = jnp.logical_and(t != n_lat - 1, t != n_lat + n_ctx - 1).astype(F32)
    keep = jnp.where(lax.broadcasted_iota(jnp.int32, (2 * SUBLANES, 1), 0) < SUBLANES, has_prev, has_next)
    gain = gain_ref[...]
    mods = st.read_mod(modl_ref, modc_ref)
    xs = [x_ref[r] for r in range(nb)]
    h = jnp.concatenate([_norm_mod(x, gain, m[:, 4 * d:5 * d], m[:, 3 * d:4 * d]).astype(BF16)
                         for x, m in zip(xs, mods)], axis=0)
    halos = [(_norm_mod(jnp.concatenate([prev_ref[r], next_ref[r]], axis=0), gain, m[:, 4 * d:5 * d],
                        m[:, 3 * d:4 * d]) * keep).astype(BF16) for r, m in enumerate(mods)]
    h_ext = jnp.concatenate([h] + halos, axis=0)
    f = wd_ref.shape[0]
    row = lax.broadcasted_iota(jnp.int32, (TILE, 1), 0)
    for c in range(f // FFN_COLS):
        cs = slice(c * FFN_COLS, (c + 1) * FFN_COLS)
        ge = _dot(h_ext, wu_ref[:, cs])
        val = _dot(h, wu_ref[:, f + c * FFN_COLS:f + (c + 1) * FFN_COLS])
        for r in range(nb):
            g = ge[r * TILE:(r + 1) * TILE]
            hb = nb * TILE + 2 * SUBLANES * r
            before = ge[hb + SUBLANES - 1:hb + SUBLANES]
            after = ge[hb + SUBLANES:hb + SUBLANES + 1]
            up = jnp.where(row == 0, before, pltpu.roll(g, 1, axis=0))
            dn = jnp.where(row == TILE - 1, after, pltpu.roll(g, TILE - 1, axis=0))
            conv = up * cw_ref[0:1, cs] + g * cw_ref[1:2, cs] + dn * cw_ref[2:3, cs] + cb_ref[:, cs]
            u_ref[r * TILE:(r + 1) * TILE, cs] = (_silu(conv) * val[r * TILE:(r + 1) * TILE]).astype(BF16)
    acc = _dot(u_ref[...], wd_ref[...])
    for r in range(nb):
        y_ref[r] = xs[r] + mods[r][:, 5 * d:6 * d] * acc[r * TILE:(r + 1) * TILE]


def _ffn(st, xs, mod, gain, w_up, conv_w, conv_b, w_down):
    d = st.d
    f = w_down.shape[0]
    per_tile = TILE // SUBLANES
    last = st.rows // SUBLANES - 1
    return pl.pallas_call(
        functools.partial(_ffn_kernel, st=st),
        grid=st.grid,
        in_specs=[st.tile(d),
                  pl.BlockSpec((st.nb, SUBLANES, d), lambda b, t: (b, jnp.maximum(t * per_tile - 1, 0), 0)),
                  pl.BlockSpec((st.nb, SUBLANES, d), lambda b, t: (b, jnp.minimum((t + 1) * per_tile, last), 0)),
                  *st.mod(), st.row(d), _resident(w_up.shape), _resident(conv_w.shape), _resident(conv_b.shape),
                  _resident(w_down.shape)],
        out_specs=st.tile(d),
        out_shape=jax.ShapeDtypeStruct((st.batch, st.rows, d), F32),
        scratch_shapes=[pltpu.VMEM((st.nb * TILE, f), BF16)],
        compiler_params=_params(("parallel", "parallel")),
        name="ffn",
    )(xs, xs, xs, mod, mod, gain, w_up, conv_w, conv_b, w_down)


def _rope_tables(seq, ctx):
    rows = seq // GRID_W
    row = jnp.repeat(jnp.arange(rows, dtype=F32), GRID_W)
    col = jnp.tile(jnp.arange(GRID_W, dtype=F32), rows)
    n_pairs = HEAD_DIM // 4
    inv = ROPE_THETA ** (-jnp.arange(n_pairs, dtype=F32) / n_pairs)
    ang_r, ang_c = row[:, None] * inv, col[:, None] * inv
    cos = jnp.concatenate([jnp.cos(ang_r)] * 2 + [jnp.cos(ang_c)] * 2, axis=1)
    sin = jnp.concatenate([-jnp.sin(ang_r), jnp.sin(ang_r), -jnp.sin(ang_c), jnp.sin(ang_c)], axis=1)
    cos = jnp.concatenate([cos, jnp.ones((ctx, HEAD_DIM), F32)], axis=0)
    sin = jnp.concatenate([sin, jnp.zeros((ctx, HEAD_DIM), F32)], axis=0)
    return jnp.tile(cos, (1, 4)), jnp.tile(sin, (1, 4))


def _head_block_constants():
    blk = 4 * HEAD_DIM
    i = np.arange(blk)
    gm = (i[:, None] // HEAD_DIM == i[None, :] // HEAD_DIM).astype(np.float32) / HEAD_DIM
    quarter = HEAD_DIM // 4
    partner = np.where((i % (2 * quarter)) < quarter, i + quarter, i - quarter)
    rot = (i[:, None] == partner[None, :]).astype(np.float32)
    return jnp.asarray(gm, BF16), jnp.asarray(rot, BF16)


def _query_head_order():
    return np.array([kh * GROUP + g for g in range(GROUP) for kh in range(N_KV_HEADS)])


def kernel(x, c, ctx, c_ctx, ada_w, ada_b, norm1_g, norm2_g, attn_w_in, attn_w_out, attn_q_gain, attn_k_gain,
           attn_sink, hgrn_w_in, hgrn_w_out, hgrn_o_gain, hgrn_lb_logits, ffn_w_up, ffn_conv_w, ffn_conv_b,
           ffn_w_down):
    batch, seq, d = x.shape
    n_ctx_tok = ctx.shape[1]
    depth = ada_w.shape[0]
    assert seq % TILE == 0 and n_ctx_tok % TILE == 0 and seq >= QBLK + 2 * WINDOW
    st = _Stream(batch, seq // TILE, n_ctx_tok // TILE, d)

    rows = -(-(batch + 1) // SUBLANES) * SUBLANES
    cc = jnp.zeros((rows, d), F32).at[:batch].set(c).at[batch].set(c_ctx)
    mod_all = _modulation(cc, ada_w, ada_b)[:, :batch + 1].reshape(depth, batch + 1, 1, 6 * d)

    cos, sin = _rope_tables(seq, n_ctx_tok)
    gm, rot = _head_block_constants()
    order = _query_head_order()
    nq = N_HEADS * HEAD_DIM
    lvl_f, lvl_b = jnp.asarray(_level_table(False), BF16), jnp.asarray(_level_table(True), BF16)

    st_lat = _Stream(batch, seq // TILE, 0, d)
    lat, con = x, ctx
    for layer in range(depth):
        j = layer // 2
        mod = mod_all[layer]
        g1 = norm1_g[layer].reshape(1, d)
        st_out = st_lat if layer == depth - 1 else st
        if layer % 2 == 0:
            w_in = attn_w_in[j]
            wq = w_in[:, :nq].reshape(d, N_HEADS, HEAD_DIM)[:, order].reshape(d, nq)
            w_in = jnp.concatenate([wq, w_in[:, nq:]], axis=1).astype(BF16)
            w_out = attn_w_out[j].reshape(N_HEADS, HEAD_DIM, d)[order].reshape(nq, d).astype(BF16)
            qg = jnp.tile(attn_q_gain[j], 4).reshape(1, 4 * HEAD_DIM)
            kg = jnp.tile(attn_k_gain[j], 4).reshape(1, 4 * HEAD_DIM)
            q, k, v = _attn_proj(st, lat, con, mod, g1, w_in, qg, kg, cos, sin, gm, rot)
            xs = _attention(attn_sink[j][order], lat, con, mod, q, k, v, w_out, seq, n_ctx_tok)
        else:
            q, kf, bf, kb, bb, v, gate = _hgrn_proj(st, lat, mod, g1, hgrn_w_in[j].astype(BF16), hgrn_lb_logits,
                                                    layer)
            o_f = _hgrn_scan_fwd(lvl_f, q, kf, bf, v, seq, n_ctx_tok)
            xs = _hgrn_scan_bwd_out(lvl_b, q, kb, bb, v, o_f, gate, lat, mod, hgrn_o_gain[j].reshape(1, HG_DK),
                                    hgrn_w_out[j].astype(BF16), seq, n_ctx_tok)
        xs = _ffn(st_out, xs, mod, norm2_g[layer].reshape(1, d), ffn_w_up[layer].astype(BF16), ffn_conv_w[layer],
                  ffn_conv_b[layer].reshape(1, -1), ffn_w_down[layer].astype(BF16))
        lat = con = xs
    return xs
```

```python
import functools
import math

import jax
import jax.numpy as jnp
import numpy as np
from jax import lax
from jax.experimental import pallas as pl
from jax.experimental.pallas import tpu as pltpu

F32 = jnp.float32
BF16 = jnp.bfloat16

EPS = 1e-6
HEAD_DIM = 64
N_HEADS = 16
N_KV_HEADS = 4
GROUP = N_HEADS // N_KV_HEADS
WINDOW = 128
ROPE_THETA = 10000.0
GRID_W = 64
HG_HEADS = 8
HG_DK = 128
CHUNK = 128
N_LEVELS = 7
TILE = 256
STREAM_ROWS = 2
QBLK = 128
ATTN_ROWS = 2
LANES = 128
SUBLANES = 8
NEG_BIG = -1e30
LOG2E = math.log2(math.e)
VMEM_LIMIT = 56 * 1024 * 1024


def _silu(x):
    return x / (1.0 + jnp.exp(-x))


def _sigmoid(x):
    return 1.0 / (1.0 + jnp.exp(-x))


def _dot(a, b):
    return jnp.dot(a, b, preferred_element_type=F32)


def _dot_nt(a, b):
    return lax.dot_general(a, b, (((1,), (1,)), ((), ())), preferred_element_type=F32)


def _dot_tn(a, b):
    return lax.dot_general(a, b, (((0,), (0,)), ((), ())), preferred_element_type=F32)


def _norm_mod(x, gain, scale, shift):
    ms = jnp.mean(x * x, axis=-1, keepdims=True)
    return (x * lax.rsqrt(ms + EPS)) * gain * (1.0 + scale) + shift


def _params(sem):
    return pltpu.CompilerParams(dimension_semantics=sem, vmem_limit_bytes=VMEM_LIMIT)


def _resident(shape):
    nd = len(shape)
    return pl.BlockSpec(shape, lambda *_: (0,) * nd, pipeline_mode=pl.Buffered(1))


def _mod_kernel(cc_ref, w_ref, b_ref, o_ref):
    a = _silu(cc_ref[...]).astype(BF16)
    o_ref[...] = _dot(a, w_ref[...].astype(BF16)) + b_ref[...]


def _modulation(cc, ada_w, ada_b):
    depth, d, n = ada_w.shape
    r = cc.shape[0]
    tn = 1536
    return pl.pallas_call(
        _mod_kernel,
        grid=(depth, n // tn),
        in_specs=[
            pl.BlockSpec((r, d), lambda l, j: (0, 0)),
            pl.BlockSpec((None, d, tn), lambda l, j: (l, 0, j)),
            pl.BlockSpec((None, 1, tn), lambda l, j: (l, 0, j)),
        ],
        out_specs=pl.BlockSpec((None, r, tn), lambda l, j: (l, 0, j)),
        out_shape=jax.ShapeDtypeStruct((depth, r, n), F32),
        compiler_params=_params(("parallel", "parallel")),
        name="modulation",
    )(cc, ada_w, ada_b.reshape(depth, 1, n))


class _Stream:
    def __init__(self, batch, n_lat, n_ctx, d, nb=STREAM_ROWS):
        self.batch, self.n_lat, self.n_ctx, self.d = batch, n_lat, n_ctx, d
        self.nb = nb if batch % nb == 0 else 1
        self.grid = (batch // self.nb, n_lat + n_ctx)
        self.rows = (n_lat + n_ctx) * TILE

    def tile(self, width, col=0):
        return pl.BlockSpec((self.nb, TILE, width), lambda b, t: (b, t, col))

    def sources(self, lat, ctx):
        n_lat = self.n_lat
        off = n_lat if ctx is lat else 0
        return [pl.BlockSpec((self.nb, TILE, self.d), lambda b, t: (b, jnp.minimum(t, n_lat - 1), 0)),
                pl.BlockSpec((self.nb, TILE, self.d), lambda b, t: (b, off + jnp.maximum(t - n_lat, 0), 0))]

    def is_ctx(self):
        return pl.program_id(1) >= self.n_lat

    def read(self, lat_ref, ctx_ref):
        return [jnp.where(self.is_ctx(), ctx_ref[r], lat_ref[r]) for r in range(self.nb)]

    def mod(self):
        return [pl.BlockSpec((self.nb, 1, 6 * self.d), lambda b, t: (b, 0, 0)),
                pl.BlockSpec((None, 1, 6 * self.d), lambda b, t: (self.batch, 0, 0))]

    def read_mod(self, lat_ref, ctx_ref):
        return [jnp.where(self.is_ctx(), ctx_ref[...], lat_ref[r]) for r in range(self.nb)]

    def row(self, width):
        return pl.BlockSpec((1, width), lambda b, t: (0, 0))


def _attn_proj_kernel(x_ref, c_ref, modl_ref, modc_ref, gain_ref, w_ref, qg_ref, kg_ref, cos_ref, sin_ref, gm_ref,
                      rot_ref, q_ref, k_ref, v_ref, *, st):
    d, nb = st.d, st.nb
    mods = st.read_mod(modl_ref, modc_ref)
    h = jnp.concatenate([_norm_mod(x, gain_ref[...], m[:, d:2 * d], m[:, 0:d]).astype(BF16)
                         for x, m in zip(st.read(x_ref, c_ref), mods)], axis=0)
    acc = _dot(h, w_ref[...])
    cos = jnp.concatenate([cos_ref[...]] * nb, axis=0)
    sin = jnp.concatenate([sin_ref[...]] * nb, axis=0)
    gm, rot = gm_ref[...], rot_ref[...]
    blk = 4 * HEAD_DIM
    n_qblk = N_HEADS * HEAD_DIM // blk

    def put(ref, cs, val):
        for r in range(nb):
            ref[r, :, cs] = val[r * TILE:(r + 1) * TILE]

    for c in range(n_qblk + 1):
        xc = acc[:, c * blk:(c + 1) * blk]
        ms = _dot((xc * xc).astype(BF16), gm)
        gain = qg_ref[...] if c < n_qblk else kg_ref[...]
        xn = xc * lax.rsqrt(ms + EPS) * gain
        y = xn * cos + _dot(xn.astype(BF16), rot) * sin
        if c < n_qblk:
            put(q_ref, slice(c * blk, (c + 1) * blk), (y * (HEAD_DIM ** -0.5 * LOG2E)).astype(BF16))
        else:
            put(k_ref, slice(None), y.astype(BF16))
    put(v_ref, slice(None), acc[:, (n_qblk + 1) * blk:].astype(BF16))


def _attn_proj(st, lat, ctx, mod, gain, w, qg, kg, cos, sin, gm, rot):
    b, d = st.batch, st.d
    t = st.rows
    nq, nk = N_HEADS * HEAD_DIM, N_KV_HEADS * HEAD_DIM
    return pl.pallas_call(
        functools.partial(_attn_proj_kernel, st=st),
        grid=st.grid,
        in_specs=[*st.sources(lat, ctx), *st.mod(), st.row(d), _resident(w.shape), st.row(nk), st.row(nk),
                  pl.BlockSpec((TILE, nk), lambda b, t: (t, 0)), pl.BlockSpec((TILE, nk), lambda b, t: (t, 0)),
                  _resident(gm.shape), _resident(rot.shape)],
        out_specs=[st.tile(nq), st.tile(nk), st.tile(nk)],
        out_shape=[jax.ShapeDtypeStruct((b, t, nq), BF16), jax.ShapeDtypeStruct((b, t, nk), BF16),
                   jax.ShapeDtypeStruct((b, t, nk), BF16)],
        compiler_params=_params(("parallel", "parallel")),
        name="attn_proj",
    )(lat, ctx, mod, mod, gain, w, qg, kg, cos, sin, gm, rot)


def _attn_kernel(sink_ref, x_ref, c_ref, modl_ref, modc_ref, q_ref, k_ref, v_ref, w_ref, y_ref, o_ref,
                 *, seq, ctx, d):
    n = pl.program_id(1)
    n_lat = seq // QBLK
    wlen = QBLK + 2 * WINDOW
    is_ctx = n >= n_lat
    ws = pl.multiple_of(jnp.clip(n * QBLK - WINDOW, 0, seq - wlen), QBLK)
    nkeys = wlen + ctx
    qpos = n * QBLK + lax.broadcasted_iota(jnp.int32, (QBLK, wlen), 0)
    kpos = ws + lax.broadcasted_iota(jnp.int32, (QBLK, wlen), 1)
    in_window = (jnp.abs(qpos - kpos) <= WINDOW) & jnp.logical_not(is_ctx)
    bias = jnp.where(in_window, 0.0, NEG_BIG).astype(F32)
    left = lax.broadcasted_iota(jnp.int32, (1, LANES), 1) < HEAD_DIM
    kv_of_lane = lax.broadcasted_iota(jnp.int32, (1, N_KV_HEADS * HEAD_DIM), 1) // HEAD_DIM
    zero = jnp.zeros((), BF16)
    blk = N_KV_HEADS * HEAD_DIM
    nb = q_ref.shape[0]
    for r, g in [(r, g) for r in range(nb) for g in range(GROUP)]:
        if g == 0:
            kall = jnp.concatenate([k_ref[r, pl.ds(ws, wlen), :], k_ref[r, seq:seq + ctx, :]], axis=0)
            vall = jnp.concatenate([v_ref[r, pl.ds(ws, wlen), :], v_ref[r, seq:seq + ctx, :]], axis=0)
            kbd = [jnp.concatenate([jnp.where(left, kall[:, p * LANES:(p + 1) * LANES], zero),
                                    jnp.where(left, zero, kall[:, p * LANES:(p + 1) * LANES])], axis=0)
                   for p in range(N_KV_HEADS // 2)]
            vbd = jnp.concatenate([jnp.where(kv_of_lane == kh, vall, zero) for kh in range(N_KV_HEADS)], axis=0)
        probs, invs = [], []
        for p in range(N_KV_HEADS // 2):
            q2 = q_ref[r, :, g * blk + p * LANES:g * blk + (p + 1) * LANES]
            s = _dot_nt(q2, kbd[p])
            for half in range(2):
                sink = sink_ref[g * N_KV_HEADS + 2 * p + half] * LOG2E
                sw = s[:, half * nkeys:half * nkeys + wlen] + bias
                sc = s[:, half * nkeys + wlen:(half + 1) * nkeys]
                m = jnp.maximum(jnp.maximum(jnp.max(sw, axis=-1, keepdims=True),
                                            jnp.max(sc, axis=-1, keepdims=True)), sink)
                pw, pc = jnp.exp2(sw - m), jnp.exp2(sc - m)
                den = jnp.sum(pw, axis=-1, keepdims=True) + jnp.sum(pc, axis=-1, keepdims=True) + jnp.exp2(sink - m)
                probs += [pw.astype(BF16), pc.astype(BF16)]
                invs.append(1.0 / den)
        o = _dot(jnp.concatenate(probs, axis=1), vbd)
        inv = jnp.where(kv_of_lane == 0, invs[0], jnp.where(kv_of_lane == 1, invs[1],
                                                            jnp.where(kv_of_lane == 2, invs[2], invs[3])))
        o_ref[r * QBLK:(r + 1) * QBLK, g * blk:(g + 1) * blk] = (o * inv).astype(BF16)
    acc = _dot(o_ref[...], w_ref[...])
    for r in range(nb):
        x = jnp.where(is_ctx, c_ref[r], x_ref[r])
        g1 = jnp.where(is_ctx, modc_ref[...], modl_ref[r])[:, 2 * d:3 * d]
        y_ref[r] = x + g1 * acc[r * QBLK:(r + 1) * QBLK]


def _attention(sink, lat, con, mod, q, k, v, w_out, seq, ctx):
    b, t, nq = q.shape
    nk = k.shape[-1]
    d = w_out.shape[1]
    n_lat = seq // QBLK
    off = n_lat if con is lat else 0
    nb = ATTN_ROWS if b % ATTN_ROWS == 0 else 1
    return pl.pallas_call(
        functools.partial(_attn_kernel, seq=seq, ctx=ctx, d=d),
        grid=(b // nb, t // QBLK),
        in_specs=[pl.BlockSpec(memory_space=pltpu.SMEM),
                  pl.BlockSpec((nb, QBLK, d), lambda b, n: (b, jnp.minimum(n, n_lat - 1), 0)),
                  pl.BlockSpec((nb, QBLK, d), lambda b, n: (b, off + jnp.maximum(n - n_lat, 0), 0)),
                  pl.BlockSpec((nb, 1, 6 * d), lambda b, n: (b, 0, 0)),
                  pl.BlockSpec((None, 1, 6 * d), lambda b, n: (mod.shape[0] - 1, 0, 0)),
                  pl.BlockSpec((nb, QBLK, nq), lambda b, n: (b, n, 0)),
                  pl.BlockSpec((nb, t, nk), lambda b, n: (b, 0, 0)),
                  pl.BlockSpec((nb, t, nk), lambda b, n: (b, 0, 0)),
                  _resident(w_out.shape)],
        out_specs=pl.BlockSpec((nb, QBLK, d), lambda b, n: (b, n, 0)),
        out_shape=jax.ShapeDtypeStruct((b, t, d), F32),
        scratch_shapes=[pltpu.VMEM((nb * QBLK, nq), BF16)],
        compiler_params=_params(("parallel", "parallel")),
        name="attention",
    )(sink, lat, con, mod, mod, q, k, v, w_out)


PROJ_COLS = 256
PROJ_AHEAD = 2


def _chunk_cumsum(g, reverse):
    n = g.shape[-1]
    groups = CHUNK // SUBLANES
    g = g.reshape(groups, SUBLANES, n)
    r = lax.broadcasted_iota(jnp.int32, g.shape, 1)
    d = 1
    while d < SUBLANES:
        if reverse:
            g = g + jnp.where(r < SUBLANES - d, pltpu.roll(g, SUBLANES - d, axis=1), 0.0)
        else:
            g = g + jnp.where(r >= d, pltpu.roll(g, d, axis=1), 0.0)
        d *= 2
    g = g.reshape(CHUNK, n)
    edge = 0 if reverse else SUBLANES - 1
    parts = [None] * groups
    carry = None
    for j in (reversed(range(groups)) if reverse else range(groups)):
        blk = g[j * SUBLANES:(j + 1) * SUBLANES]
        if carry is not None:
            blk = blk + carry
        parts[j] = blk
        carry = blk[edge:edge + 1]
    return jnp.concatenate(parts, axis=0)


def _hgrn_proj_kernel(x_ref, modl_ref, modc_ref, gain_ref, w_ref, lbl_ref, q_ref, kf_ref, bf_ref, kb_ref, bb_ref,
                      v_ref, gate_ref, acc_ref, *, st, layer):
    d, nb = st.d, st.nb
    h = jnp.concatenate([_norm_mod(x_ref[r], gain_ref[...], m[:, d:2 * d], m[:, 0:d]).astype(BF16)
                         for r, m in enumerate(st.read_mod(modl_ref, modc_ref))], axis=0)
    lg = lbl_ref[...]
    e = jnp.exp(lg - jnp.max(lg, axis=0, keepdims=True))
    lb = jnp.sum(e[1:layer + 1], axis=0, keepdims=True) / jnp.sum(e, axis=0, keepdims=True)
    w = HG_HEADS * HG_DK

    def put(ref, cs, val):
        for r in range(nb):
            ref[r, :, cs] = val[r * TILE:(r + 1) * TILE]

    def finish(seg, c, acc):
        cs = slice(c * PROJ_COLS, (c + 1) * PROJ_COLS)
        if seg == 0:
            put(q_ref, cs, _silu(acc).astype(BF16))
        elif seg in (1, 2):
            k_ref, b_ref = (kf_ref, bf_ref) if seg == 1 else (kb_ref, bb_ref)
            f = lb[:, cs] + (1.0 - lb[:, cs]) * _sigmoid(acc)
            put(k_ref, cs, (1.0 - f).astype(BF16))
            g = jnp.log2(f)
            for r in range(nb):
                for j in range(TILE // CHUNK):
                    rows = slice(r * TILE + j * CHUNK, r * TILE + (j + 1) * CHUNK)
                    b_ref[r, j * CHUNK:(j + 1) * CHUNK, cs] = _chunk_cumsum(g[rows], reverse=seg == 2)
        elif seg == 3:
            put(v_ref, cs, acc.astype(BF16))
        else:
            put(gate_ref, cs, _silu(acc).astype(BF16))

    base = jnp.minimum(pl.program_id(0), 0)
    items = [(seg, c) for c in range(w // PROJ_COLS) for seg in (1, 2, 0, 4, 3)]
    slots = acc_ref.shape[0]
    for i in range(len(items) + PROJ_AHEAD):
        if i < len(items):
            seg, c = items[i]
            acc_ref[base + i % slots] = _dot(h, w_ref[:, seg * w + c * PROJ_COLS:seg * w + (c + 1) * PROJ_COLS])
        if i >= PROJ_AHEAD:
            finish(*items[i - PROJ_AHEAD], acc_ref[base + (i - PROJ_AHEAD) % slots])


def _hgrn_proj(st, xs, mod, gain, w, lb_logits, layer):
    b, t, d = xs.shape
    wd = HG_HEADS * HG_DK
    assert wd == d
    shapes = [(wd, BF16), (wd, BF16), (wd, F32), (wd, BF16), (wd, F32), (d, BF16), (d, BF16)]
    return pl.pallas_call(
        functools.partial(_hgrn_proj_kernel, st=st, layer=layer),
        grid=st.grid,
        in_specs=[st.tile(d), *st.mod(), st.row(d), _resident(w.shape), _resident(lb_logits.shape)],
        out_specs=[st.tile(n) for n, _ in shapes],
        out_shape=[jax.ShapeDtypeStruct((b, t, n), dt) for n, dt in shapes],
        scratch_shapes=[pltpu.VMEM((PROJ_AHEAD + 1, st.nb * TILE, PROJ_COLS), F32)],
        compiler_params=_params(("parallel", "parallel")),
        name="hgrn_proj",
    )(xs, mod, mod, gain, w, lb_logits)


def _level_table(reverse):
    t = np.arange(CHUNK)[:, None]
    s = np.arange(CHUNK)[None, :]
    x = t ^ s
    lvl = np.floor(np.log2(np.maximum(x, 1))).astype(np.int32)
    later = (t < s) if reverse else (t > s)
    lvl = np.where(later, lvl, -1)
    for i in range(CHUNK):
        j = i if reverse else ~i & (CHUNK - 1)
        lvl[i, i] = N_LEVELS if j == 0 else (j & -j).bit_length() - 1
    return np.concatenate([lvl, lvl], axis=1).astype(np.float32)


SLAB = 16


def _neg_abs16(x):
    sign = jnp.int16(-2 ** 15)
    return lax.bitcast_convert_type(lax.bitcast_convert_type(x, jnp.int16) | sign, BF16)


def _mid_rows(b2, half, reverse):
    n = b2.shape[-1]
    blk = 2 * half
    m = half if reverse else half - 1
    if half == 1:
        b3 = b2.reshape(CHUNK // SUBLANES, SUBLANES, n)
        odd = lax.broadcasted_iota(jnp.int32, b3.shape, 1) % 2 == 1
        if reverse:
            return jnp.where(odd, b3, pltpu.roll(b3, SUBLANES - 1, axis=1)).reshape(CHUNK, n)
        return jnp.where(odd, pltpu.roll(b3, 1, axis=1), b3).reshape(CHUNK, n)
    if blk >= SUBLANES:
        b3 = b2.reshape(CHUNK // blk, blk, n)
        return jnp.broadcast_to(b3[:, m:m + 1, :], b3.shape).reshape(CHUNK, n)
    b3 = b2.reshape(CHUNK // SUBLANES, SUBLANES, n)
    r = lax.broadcasted_iota(jnp.int32, b3.shape, 1)
    mid = jnp.broadcast_to(b3[:, m:m + 1, :], b3.shape)
    for j in range(1, SUBLANES // blk):
        mid = jnp.where(r >= j * blk, jnp.broadcast_to(b3[:, j * blk + m:j * blk + m + 1, :], b3.shape), mid)
    return mid.reshape(CHUNK, n)


def _scan_chunk(lvl_ref, q_ref, k_ref, b_ref, v_ref, o_ref, st_ref, reverse):
    @pl.when(pl.program_id(1) == 0)
    def _():
        st_ref[...] = jnp.zeros(st_ref.shape, F32)

    pw = 2 * HG_DK
    lvl = lvl_ref[...]
    lane = lax.broadcasted_iota(jnp.int32, (1, pw), 1)
    same_head = (lax.broadcasted_iota(jnp.int32, (pw, pw), 0) < HG_DK) == (lane < HG_DK)
    zero = jnp.zeros((), BF16)
    edge = 0 if reverse else CHUNK - 1
    slab = slice(0, SLAB) if reverse else slice(CHUNK - SLAB, CHUNK)

    def pair_rows(x):
        z = jnp.zeros((CHUNK, HG_DK), x.dtype)
        return jnp.concatenate([jnp.concatenate([x[:, :HG_DK], z], axis=1),
                                jnp.concatenate([z, x[:, HG_DK:]], axis=1)], axis=0)

    def pair_cols(xt):
        z = jnp.zeros((HG_DK, CHUNK), xt.dtype)
        return jnp.concatenate([jnp.concatenate([xt[:HG_DK], z], axis=1),
                                jnp.concatenate([z, xt[HG_DK:]], axis=1)], axis=0)

    for r, p in [(r, p) for r in range(q_ref.shape[0]) for p in range(HG_HEADS // 2)]:
        cs = slice(p * pw, (p + 1) * pw)
        q2 = q_ref[r, :, cs]
        k2 = k_ref[r, :, cs]
        b2 = b_ref[r, :, cs]
        v2 = v_ref[r, :, cs]
        b_edge = b2[edge:edge + 1, :]
        st = st_ref[r, p]
        o = _dot_nt(q2 * jnp.exp2(b2).astype(BF16), st.astype(BF16))
        qk = q2[slab].astype(F32) * k2[slab].astype(F32)
        diag = jnp.where(lane < CHUNK, jnp.sum(qk[:, :HG_DK], axis=-1, keepdims=True),
                         jnp.sum(qk[:, HG_DK:], axis=-1, keepdims=True))
        last = jnp.where(lvl[slab] == N_LEVELS, diag.astype(BF16), zero)
        rest = jnp.zeros((CHUNK - SLAB, pw), BF16)
        a = jnp.concatenate([last, rest] if reverse else [rest, last], axis=0)
        kt = k2.T
        for l in range(N_LEVELS):
            e = jnp.exp2(_neg_abs16((b2 - _mid_rows(b2, 2 ** l, reverse)).astype(BF16)))
            a_l = _dot(q2 * e, pair_cols(kt * e.T))
            a = jnp.where(lvl == l, a_l.astype(BF16), a)
        o = o + _dot(a, pair_rows(v2))
        o_ref[r, :, cs] = o.astype(o_ref.dtype)
        kdec = k2 * jnp.exp2(b_edge - b2).astype(BF16)
        upd = _dot_tn(v2, kdec)
        st_ref[r, p] = st * jnp.exp2(b_edge) + jnp.where(same_head, upd, 0.0)


def _scan_fwd_kernel(lvl_ref, q_ref, k_ref, b_ref, v_ref, o_ref, st_ref):
    _scan_chunk(lvl_ref, q_ref, k_ref, b_ref, v_ref, o_ref, st_ref, reverse=False)


def _scan_bwd_out_kernel(lvl_ref, q_ref, k_ref, b_ref, v_ref, of_ref, gate_ref, x_ref, modl_ref, modc_ref, og_ref,
                         w_ref, y_ref, st_ref, ob_ref, *, n_ctx, d):
    _scan_chunk(lvl_ref, q_ref, k_ref, b_ref, v_ref, ob_ref, st_ref, reverse=True)
    rows = q_ref.shape[0]
    og = og_ref[...]
    parts = []
    for r in range(rows):
        o = of_ref[r].astype(F32) + ob_ref[r]
        heads = []
        for hd in range(HG_HEADS):
            oh = o[:, hd * HG_DK:(hd + 1) * HG_DK]
            ms = jnp.mean(oh * oh, axis=-1, keepdims=True)
            heads.append(oh * lax.rsqrt(ms + EPS) * og)
        parts.append((jnp.concatenate(heads, axis=1) * gate_ref[r].astype(F32)).astype(BF16))
    acc = _dot(jnp.concatenate(parts, axis=0), w_ref[...])
    is_ctx = pl.program_id(1) < n_ctx
    for r in range(rows):
        g1 = jnp.where(is_ctx, modc_ref[...], modl_ref[r])[:, 2 * d:3 * d]
        y_ref[r] = x_ref[r] + g1 * acc[r * CHUNK:(r + 1) * CHUNK]


SCAN_ROWS = 2


def _scan_specs(b, w, seq, ctx, reverse):
    n_lat, n_ctx = seq // CHUNK, ctx // CHUNK
    n = n_lat + n_ctx

    def chunk(i):
        if reverse:
            return n - 1 - i
        return jnp.where(i < n_ctx, n_lat + i, i - n_ctx)

    rows = SCAN_ROWS if b % SCAN_ROWS == 0 else 1
    return rows, n, pl.BlockSpec((rows, CHUNK, w), lambda b, i: (b, chunk(i), 0))


def _hgrn_scan_fwd(lvl, q, k, bcum, v, seq, ctx):
    b, t, w = q.shape
    rows, n, spec = _scan_specs(b, w, seq, ctx, reverse=False)
    return pl.pallas_call(
        _scan_fwd_kernel,
        grid=(b // rows, n),
        in_specs=[_resident(lvl.shape), spec, spec, spec, spec],
        out_specs=spec,
        out_shape=jax.ShapeDtypeStruct((b, t, w), BF16),
        scratch_shapes=[pltpu.VMEM((rows, HG_HEADS // 2, 2 * HG_DK, 2 * HG_DK), F32)],
        compiler_params=_params(("parallel", "arbitrary")),
        name="hgrn_scan_fwd",
    )(lvl, q, k, bcum, v)


def _hgrn_scan_bwd_out(lvl, q, k, bcum, v, o_f, gate, xs, mod, o_gain, w_out, seq, ctx):
    b, t, w = q.shape
    d = w_out.shape[1]
    assert w == d
    rows, n, spec = _scan_specs(b, w, seq, ctx, reverse=True)
    return pl.pallas_call(
        functools.partial(_scan_bwd_out_kernel, n_ctx=ctx // CHUNK, d=d),
        grid=(b // rows, n),
        in_specs=[_resident(lvl.shape), spec, spec, spec, spec, spec, spec, spec,
                  pl.BlockSpec((rows, 1, 6 * d), lambda b, i: (b, 0, 0)),
                  pl.BlockSpec((None, 1, 6 * d), lambda b, i: (mod.shape[0] - 1, 0, 0)),
                  pl.BlockSpec((1, HG_DK), lambda b, i: (0, 0)), _resident(w_out.shape)],
        out_specs=spec,
        out_shape=jax.ShapeDtypeStruct((b, t, d), F32),
        scratch_shapes=[pltpu.VMEM((rows, HG_HEADS // 2, 2 * HG_DK, 2 * HG_DK), F32),
                        pltpu.VMEM((rows, CHUNK, w), F32)],
        compiler_params=_params(("parallel", "arbitrary")),
        name="hgrn_scan_bwd_out",
    )(lvl, q, k, bcum, v, o_f, gate, xs, mod, mod, o_gain, w_out)


FFN_COLS = 256
FFN_ROWS = 4


def _ffn_kernel(x_ref, prev_ref, next_ref, modl_ref, modc_ref, gain_ref, wu_ref, cw_ref, cb_ref, wd_ref, y_ref, u_ref,
                *, st):
    d, nb, n_lat, n_ctx = st.d, st.nb, st.n_lat, st.n_ctx
    t = pl.program_id(1)
    has_prev = jnp.logical_and(t != 0, t != n_lat).astype(F32)
    has_next = jnp.logical_and(t != n_lat - 1, t != n_lat + n_ctx - 1).astype(F32)
    keep = jnp.where(lax.broadcasted_iota(jnp.int32, (2 * SUBLANES, 1), 0) < SUBLANES, has_prev, has_next)
    gain = gain_ref[...]
    mods = st.read_mod(modl_ref, modc_ref)
    xs = [x_ref[r] for r in range(nb)]
    h = jnp.concatenate([_norm_mod(x, gain, m[:, 4 * d:5 * d], m[:, 3 * d:4 * d]).astype(BF16)
                         for x, m in zip(xs, mods)], axis=0)
    halos = [(_norm_mod(jnp.concatenate([prev_ref[r], next_ref[r]], axis=0), gain, m[:, 4 * d:5 * d],
                        m[:, 3 * d:4 * d]) * keep).astype(BF16) for r, m in enumerate(mods)]
    h_ext = jnp.concatenate([h] + halos, axis=0)
    f = wd_ref.shape[0]
    row = lax.broadcasted_iota(jnp.int32, (TILE, 1), 0)
    for c in range(f // FFN_COLS):
        cs = slice(c * FFN_COLS, (c + 1) * FFN_COLS)
        ge = _dot(h_ext, wu_ref[:, cs])
        val = _dot(h, wu_ref[:, f + c * FFN_COLS:f + (c + 1) * FFN_COLS])
        for r in range(nb):
            g = ge[r * TILE:(r + 1) * TILE]
            hb = nb * TILE + 2 * SUBLANES * r
            before = ge[hb + SUBLANES - 1:hb + SUBLANES]
            after = ge[hb + SUBLANES:hb + SUBLANES + 1]
            up = jnp.where(row == 0, before, pltpu.roll(g, 1, axis=0))
            dn = jnp.where(row == TILE - 1, after, pltpu.roll(g, TILE - 1, axis=0))
            conv = up * cw_ref[0:1, cs] + g * cw_ref[1:2, cs] + dn * cw_ref[2:3, cs] + cb_ref[:, cs]
            u_ref[r * TILE:(r + 1) * TILE, cs] = (_silu(conv) * val[r * TILE:(r + 1) * TILE]).astype(BF16)
    acc = _dot(u_ref[...], wd_ref[...])
    for r in range(nb):
        y_ref[r] = xs[r] + mods[r][:, 5 * d:6 * d] * acc[r * TILE:(r + 1) * TILE]


def _ffn(st, xs, mod, gain, w_up, conv_w, conv_b, w_down):
    d = st.d
    f = w_down.shape[0]
    per_tile = TILE // SUBLANES
    last = st.rows // SUBLANES - 1
    return pl.pallas_call(
        functools.partial(_ffn_kernel, st=st),
        grid=st.grid,
        in_specs=[st.tile(d),
                  pl.BlockSpec((st.nb, SUBLANES, d), lambda b, t: (b, jnp.maximum(t * per_tile - 1, 0), 0)),
                  pl.BlockSpec((st.nb, SUBLANES, d), lambda b, t: (b, jnp.minimum((t + 1) * per_tile, last), 0)),
                  *st.mod(), st.row(d), _resident(w_up.shape), _resident(conv_w.shape), _resident(conv_b.shape),
                  _resident(w_down.shape)],
        out_specs=st.tile(d),
        out_shape=jax.ShapeDtypeStruct((st.batch, st.rows, d), F32),
        scratch_shapes=[pltpu.VMEM((st.nb * TILE, f), BF16)],
        compiler_params=_params(("parallel", "parallel")),
        name="ffn",
    )(xs, xs, xs, mod, mod, gain, w_up, conv_w, conv_b, w_down)


def _rope_tables(seq, ctx):
    rows = seq // GRID_W
    row = jnp.repeat(jnp.arange(rows, dtype=F32), GRID_W)
    col = jnp.tile(jnp.arange(GRID_W, dtype=F32), rows)
    n_pairs = HEAD_DIM // 4
    inv = ROPE_THETA ** (-jnp.arange(n_pairs, dtype=F32) / n_pairs)
    ang_r, ang_c = row[:, None] * inv, col[:, None] * inv
    cos = jnp.concatenate([jnp.cos(ang_r)] * 2 + [jnp.cos(ang_c)] * 2, axis=1)
    sin = jnp.concatenate([-jnp.sin(ang_r), jnp.sin(ang_r), -jnp.sin(ang_c), jnp.sin(ang_c)], axis=1)
    cos = jnp.concatenate([cos, jnp.ones((ctx, HEAD_DIM), F32)], axis=0)
    sin = jnp.concatenate([sin, jnp.zeros((ctx, HEAD_DIM), F32)], axis=0)
    return jnp.tile(cos, (1, 4)), jnp.tile(sin, (1, 4))


def _head_block_constants():
    blk = 4 * HEAD_DIM
    i = np.arange(blk)
    gm = (i[:, None] // HEAD_DIM == i[None, :] // HEAD_DIM).astype(np.float32) / HEAD_DIM
    quarter = HEAD_DIM // 4
    partner = np.where((i % (2 * quarter)) < quarter, i + quarter, i - quarter)
    rot = (i[:, None] == partner[None, :]).astype(np.float32)
    return jnp.asarray(gm, BF16), jnp.asarray(rot, BF16)


def _query_head_order():
    return np.array([kh * GROUP + g for g in range(GROUP) for kh in range(N_KV_HEADS)])


def kernel(x, c, ctx, c_ctx, ada_w, ada_b, norm1_g, norm2_g, attn_w_in, attn_w_out, attn_q_gain, attn_k_gain,
           attn_sink, hgrn_w_in, hgrn_w_out, hgrn_o_gain, hgrn_lb_logits, ffn_w_up, ffn_conv_w, ffn_conv_b,
           ffn_w_down):
    batch, seq, d = x.shape
    n_ctx_tok = ctx.shape[1]
    depth = ada_w.shape[0]
    assert seq % TILE == 0 and n_ctx_tok % TILE == 0 and seq >= QBLK + 2 * WINDOW
    st = _Stream(batch, seq // TILE, n_ctx_tok // TILE, d)

    rows = -(-(batch + 1) // SUBLANES) * SUBLANES
    cc = jnp.zeros((rows, d), F32).at[:batch].set(c).at[batch].set(c_ctx)
    mod_all = _modulation(cc, ada_w, ada_b)[:, :batch + 1].reshape(depth, batch + 1, 1, 6 * d)

    cos, sin = _rope_tables(seq, n_ctx_tok)
    gm, rot = _head_block_constants()
    order = _query_head_order()
    nq = N_HEADS * HEAD_DIM
    lvl_f, lvl_b = jnp.asarray(_level_table(False), BF16), jnp.asarray(_level_table(True), BF16)

    st_ffn = _Stream(batch, seq // TILE, n_ctx_tok // TILE, d, FFN_ROWS)
    st_ffn_lat = _Stream(batch, seq // TILE, 0, d, FFN_ROWS)
    lat, con = x, ctx
    for layer in range(depth):
        j = layer // 2
        mod = mod_all[layer]
        g1 = norm1_g[layer].reshape(1, d)
        st_out = st_ffn_lat if layer == depth - 1 else st_ffn
        if layer % 2 == 0:
            w_in = attn_w_in[j]
            wq = w_in[:, :nq].reshape(d, N_HEADS, HEAD_DIM)[:, order].reshape(d, nq)
            w_in = jnp.concatenate([wq, w_in[:, nq:]], axis=1).astype(BF16)
            w_out = attn_w_out[j].reshape(N_HEADS, HEAD_DIM, d)[order].reshape(nq, d).astype(BF16)
            qg = jnp.tile(attn_q_gain[j], 4).reshape(1, 4 * HEAD_DIM)
            kg = jnp.tile(attn_k_gain[j], 4).reshape(1, 4 * HEAD_DIM)
            q, k, v = _attn_proj(st, lat, con, mod, g1, w_in, qg, kg, cos, sin, gm, rot)
            xs = _attention(attn_sink[j][order], lat, con, mod, q, k, v, w_out, seq, n_ctx_tok)
        else:
            q, kf, bf, kb, bb, v, gate = _hgrn_proj(st, lat, mod, g1, hgrn_w_in[j].astype(BF16), hgrn_lb_logits,
                                                    layer)
            o_f = _hgrn_scan_fwd(lvl_f, q, kf, bf, v, seq, n_ctx_tok)
            xs = _hgrn_scan_bwd_out(lvl_b, q, kb, bb, v, o_f, gate, lat, mod, hgrn_o_gain[j].reshape(1, HG_DK),
                                    hgrn_w_out[j].astype(BF16), seq, n_ctx_tok)
        xs = _ffn(st_out, xs, mod, norm2_g[layer].reshape(1, d), ffn_w_up[layer].astype(BF16), ffn_conv_w[layer],
                  ffn_conv_b[layer].reshape(1, -1), ffn_w_down[layer].astype(BF16))
        lat = con = xs
    return xs
```

```python
import functools
import math

import jax
import jax.numpy as jnp
import numpy as np
from jax import lax
from jax.experimental import pallas as pl
from jax.experimental.pallas import tpu as pltpu

F32 = jnp.float32
BF16 = jnp.bfloat16

EPS = 1e-6
HEAD_DIM = 64
N_HEADS = 16
N_KV_HEADS = 4
GROUP = N_HEADS // N_KV_HEADS
WINDOW = 128
ROPE_THETA = 10000.0
GRID_W = 64
HG_HEADS = 8
HG_DK = 128
CHUNK = 128
N_LEVELS = 7
TILE = 256
STREAM_ROWS = 2
QBLK = 128
ATTN_ROWS = 2
LANES = 128
SUBLANES = 8
NEG_BIG = -1e30
LOG2E = math.log2(math.e)
VMEM_LIMIT = 56 * 1024 * 1024


def _silu(x):
    return x / (1.0 + jnp.exp(-x))


def _sigmoid(x):
    return 1.0 / (1.0 + jnp.exp(-x))


def _dot(a, b):
    return jnp.dot(a, b, preferred_element_type=F32)


def _dot_nt(a, b):
    return lax.dot_general(a, b, (((1,), (1,)), ((), ())), preferred_element_type=F32)


def _dot_tn(a, b):
    return lax.dot_general(a, b, (((0,), (0,)), ((), ())), preferred_element_type=F32)


def _norm_mod(x, gain, scale, shift):
    ms = jnp.mean(x * x, axis=-1, keepdims=True)
    return (x * lax.rsqrt(ms + EPS)) * gain * (1.0 + scale) + shift


def _params(sem):
    return pltpu.CompilerParams(dimension_semantics=sem, vmem_limit_bytes=VMEM_LIMIT)


def _resident(shape):
    nd = len(shape)
    return pl.BlockSpec(shape, lambda *_: (0,) * nd, pipeline_mode=pl.Buffered(1))


def _mod_kernel(cc_ref, w_ref, b_ref, o_ref):
    a = _silu(cc_ref[...]).astype(BF16)
    o_ref[...] = _dot(a, w_ref[...].astype(BF16)) + b_ref[...]


def _modulation(cc, ada_w, ada_b):
    depth, d, n = ada_w.shape
    r = cc.shape[0]
    tn = 1536
    return pl.pallas_call(
        _mod_kernel,
        grid=(depth, n // tn),
        in_specs=[
            pl.BlockSpec((r, d), lambda l, j: (0, 0)),
            pl.BlockSpec((None, d, tn), lambda l, j: (l, 0, j)),
            pl.BlockSpec((None, 1, tn), lambda l, j: (l, 0, j)),
        ],
        out_specs=pl.BlockSpec((None, r, tn), lambda l, j: (l, 0, j)),
        out_shape=jax.ShapeDtypeStruct((depth, r, n), F32),
        compiler_params=_params(("parallel", "parallel")),
        name="modulation",
    )(cc, ada_w, ada_b.reshape(depth, 1, n))


class _Stream:
    def __init__(self, batch, n_lat, n_ctx, d, nb=STREAM_ROWS):
        self.batch, self.n_lat, self.n_ctx, self.d = batch, n_lat, n_ctx, d
        self.nb = nb if batch % nb == 0 else 1
        self.grid = (batch // self.nb, n_lat + n_ctx)
        self.rows = (n_lat + n_ctx) * TILE

    def tile(self, width, col=0):
        return pl.BlockSpec((self.nb, TILE, width), lambda b, t: (b, t, col))

    def sources(self, lat, ctx):
        n_lat = self.n_lat
        off = n_lat if ctx is lat else 0
        return [pl.BlockSpec((self.nb, TILE, self.d), lambda b, t: (b, jnp.minimum(t, n_lat - 1), 0)),
                pl.BlockSpec((self.nb, TILE, self.d), lambda b, t: (b, off + jnp.maximum(t - n_lat, 0), 0))]

    def is_ctx(self):
        return pl.program_id(1) >= self.n_lat

    def read(self, lat_ref, ctx_ref):
        return [jnp.where(self.is_ctx(), ctx_ref[r], lat_ref[r]) for r in range(self.nb)]

    def mod(self):
        return [pl.BlockSpec((self.nb, 1, 6 * self.d), lambda b, t: (b, 0, 0)),
                pl.BlockSpec((None, 1, 6 * self.d), lambda b, t: (self.batch, 0, 0))]

    def read_mod(self, lat_ref, ctx_ref):
        return [jnp.where(self.is_ctx(), ctx_ref[...], lat_ref[r]) for r in range(self.nb)]

    def row(self, width):
        return pl.BlockSpec((1, width), lambda b, t: (0, 0))


def _attn_proj_kernel(x_ref, c_ref, modl_ref, modc_ref, gain_ref, w_ref, qg_ref, kg_ref, cos_ref, sin_ref, gm_ref,
                      rot_ref, q_ref, k_ref, v_ref, *, st):
    d, nb = st.d, st.nb
    mods = st.read_mod(modl_ref, modc_ref)
    h = jnp.concatenate([_norm_mod(x, gain_ref[...], m[:, d:2 * d], m[:, 0:d]).astype(BF16)
                         for x, m in zip(st.read(x_ref, c_ref), mods)], axis=0)
    acc = _dot(h, w_ref[...])
    cos = jnp.concatenate([cos_ref[...]] * nb, axis=0)
    sin = jnp.concatenate([sin_ref[...]] * nb, axis=0)
    gm, rot = gm_ref[...], rot_ref[...]
    blk = 4 * HEAD_DIM
    n_qblk = N_HEADS * HEAD_DIM // blk

    def put(ref, cs, val):
        for r in range(nb):
            ref[r, :, cs] = val[r * TILE:(r + 1) * TILE]

    for c in range(n_qblk + 1):
        xc = acc[:, c * blk:(c + 1) * blk]
        ms = _dot((xc * xc).astype(BF16), gm)
        gain = qg_ref[...] if c < n_qblk else kg_ref[...]
        xn = xc * lax.rsqrt(ms + EPS) * gain
        y = xn * cos + _dot(xn.astype(BF16), rot) * sin
        if c < n_qblk:
            put(q_ref, slice(c * blk, (c + 1) * blk), (y * (HEAD_DIM ** -0.5 * LOG2E)).astype(BF16))
        else:
            put(k_ref, slice(None), y.astype(BF16))
    put(v_ref, slice(None), acc[:, (n_qblk + 1) * blk:].astype(BF16))


def _attn_proj(st, lat, ctx, mod, gain, w, qg, kg, cos, sin, gm, rot):
    b, d = st.batch, st.d
    t = st.rows
    nq, nk = N_HEADS * HEAD_DIM, N_KV_HEADS * HEAD_DIM
    return pl.pallas_call(
        functools.partial(_attn_proj_kernel, st=st),
        grid=st.grid,
        in_specs=[*st.sources(lat, ctx), *st.mod(), st.row(d), _resident(w.shape), st.row(nk), st.row(nk),
                  pl.BlockSpec((TILE, nk), lambda b, t: (t, 0)), pl.BlockSpec((TILE, nk), lambda b, t: (t, 0)),
                  _resident(gm.shape), _resident(rot.shape)],
        out_specs=[st.tile(nq), st.tile(nk), st.tile(nk)],
        out_shape=[jax.ShapeDtypeStruct((b, t, nq), BF16), jax.ShapeDtypeStruct((b, t, nk), BF16),
                   jax.ShapeDtypeStruct((b, t, nk), BF16)],
        compiler_params=_params(("parallel", "parallel")),
        name="attn_proj",
    )(lat, ctx, mod, mod, gain, w, qg, kg, cos, sin, gm, rot)


def _attn_kernel(sink_ref, x_ref, c_ref, modl_ref, modc_ref, q_ref, k_ref, v_ref, w_ref, y_ref, o_ref,
                 *, seq, ctx, d):
    n = pl.program_id(1)
    n_lat = seq // QBLK
    wlen = QBLK + 2 * WINDOW
    is_ctx = n >= n_lat
    ws = pl.multiple_of(jnp.clip(n * QBLK - WINDOW, 0, seq - wlen), QBLK)
    nkeys = wlen + ctx
    qpos = n * QBLK + lax.broadcasted_iota(jnp.int32, (QBLK, wlen), 0)
    kpos = ws + lax.broadcasted_iota(jnp.int32, (QBLK, wlen), 1)
    in_window = (jnp.abs(qpos - kpos) <= WINDOW) & jnp.logical_not(is_ctx)
    bias = jnp.where(in_window, 0.0, NEG_BIG).astype(F32)
    left = lax.broadcasted_iota(jnp.int32, (1, LANES), 1) < HEAD_DIM
    kv_of_lane = lax.broadcasted_iota(jnp.int32, (1, N_KV_HEADS * HEAD_DIM), 1) // HEAD_DIM
    zero = jnp.zeros((), BF16)
    blk = N_KV_HEADS * HEAD_DIM
    nb = q_ref.shape[0]
    for r, g in [(r, g) for r in range(nb) for g in range(GROUP)]:
        if g == 0:
            kall = jnp.concatenate([k_ref[r, pl.ds(ws, wlen), :], k_ref[r, seq:seq + ctx, :]], axis=0)
            vall = jnp.concatenate([v_ref[r, pl.ds(ws, wlen), :], v_ref[r, seq:seq + ctx, :]], axis=0)
            kbd = [jnp.concatenate([jnp.where(left, kall[:, p * LANES:(p + 1) * LANES], zero),
                                    jnp.where(left, zero, kall[:, p * LANES:(p + 1) * LANES])], axis=0)
                   for p in range(N_KV_HEADS // 2)]
            vbd = jnp.concatenate([jnp.where(kv_of_lane == kh, vall, zero) for kh in range(N_KV_HEADS)], axis=0)
        probs, invs = [], []
        for p in range(N_KV_HEADS // 2):
            q2 = q_ref[r, :, g * blk + p * LANES:g * blk + (p + 1) * LANES]
            s = _dot_nt(q2, kbd[p])
            for half in range(2):
                sink = sink_ref[g * N_KV_HEADS + 2 * p + half] * LOG2E
                sw = s[:, half * nkeys:half * nkeys + wlen] + bias
                sc = s[:, half * nkeys + wlen:(half + 1) * nkeys]
                m = jnp.maximum(jnp.maximum(jnp.max(sw, axis=-1, keepdims=True),
                                            jnp.max(sc, axis=-1, keepdims=True)), sink)
                pw, pc = jnp.exp2(sw - m), jnp.exp2(sc - m)
                den = jnp.sum(pw, axis=-1, keepdims=True) + jnp.sum(pc, axis=-1, keepdims=True) + jnp.exp2(sink - m)
                probs += [pw.astype(BF16), pc.astype(BF16)]
                invs.append(1.0 / den)
        o = _dot(jnp.concatenate(probs, axis=1), vbd)
        inv = jnp.where(kv_of_lane == 0, invs[0], jnp.where(kv_of_lane == 1, invs[1],
                                                            jnp.where(kv_of_lane == 2, invs[2], invs[3])))
        o_ref[r * QBLK:(r + 1) * QBLK, g * blk:(g + 1) * blk] = (o * inv).astype(BF16)
    acc = _dot(o_ref[...], w_ref[...])
    for r in range(nb):
        x = jnp.where(is_ctx, c_ref[r], x_ref[r])
        g1 = jnp.where(is_ctx, modc_ref[...], modl_ref[r])[:, 2 * d:3 * d]
        y_ref[r] = x + g1 * acc[r * QBLK:(r + 1) * QBLK]


def _attention(sink, lat, con, mod, q, k, v, w_out, seq, ctx):
    b, t, nq = q.shape
    nk = k.shape[-1]
    d = w_out.shape[1]
    n_lat = seq // QBLK
    off = n_lat if con is lat else 0
    nb = ATTN_ROWS if b % ATTN_ROWS == 0 else 1
    return pl.pallas_call(
        functools.partial(_attn_kernel, seq=seq, ctx=ctx, d=d),
        grid=(b // nb, t // QBLK),
        in_specs=[pl.BlockSpec(memory_space=pltpu.SMEM),
                  pl.BlockSpec((nb, QBLK, d), lambda b, n: (b, jnp.minimum(n, n_lat - 1), 0)),
                  pl.BlockSpec((nb, QBLK, d), lambda b, n: (b, off + jnp.maximum(n - n_lat, 0), 0)),
                  pl.BlockSpec((nb, 1, 6 * d), lambda b, n: (b, 0, 0)),
                  pl.BlockSpec((None, 1, 6 * d), lambda b, n: (mod.shape[0] - 1, 0, 0)),
                  pl.BlockSpec((nb, QBLK, nq), lambda b, n: (b, n, 0)),
                  pl.BlockSpec((nb, t, nk), lambda b, n: (b, 0, 0)),
                  pl.BlockSpec((nb, t, nk), lambda b, n: (b, 0, 0)),
                  _resident(w_out.shape)],
        out_specs=pl.BlockSpec((nb, QBLK, d), lambda b, n: (b, n, 0)),
        out_shape=jax.ShapeDtypeStruct((b, t, d), F32),
        scratch_shapes=[pltpu.VMEM((nb * QBLK, nq), BF16)],
        compiler_params=_params(("parallel", "parallel")),
        name="attention",
    )(sink, lat, con, mod, mod, q, k, v, w_out)


PROJ_COLS = 256
PROJ_AHEAD = 2


def _chunk_cumsum(g, reverse):
    n = g.shape[-1]
    groups = CHUNK // SUBLANES
    g = g.reshape(groups, SUBLANES, n)
    r = lax.broadcasted_iota(jnp.int32, g.shape, 1)
    d = 1
    while d < SUBLANES:
        if reverse:
            g = g + jnp.where(r < SUBLANES - d, pltpu.roll(g, SUBLANES - d, axis=1), 0.0)
        else:
            g = g + jnp.where(r >= d, pltpu.roll(g, d, axis=1), 0.0)
        d *= 2
    g = g.reshape(CHUNK, n)
    edge = 0 if reverse else SUBLANES - 1
    parts = [None] * groups
    carry = None
    for j in (reversed(range(groups)) if reverse else range(groups)):
        blk = g[j * SUBLANES:(j + 1) * SUBLANES]
        if carry is not None:
            blk = blk + carry
        parts[j] = blk
        carry = blk[edge:edge + 1]
    return jnp.concatenate(parts, axis=0)


def _hgrn_proj_kernel(x_ref, modl_ref, modc_ref, gain_ref, w_ref, lbl_ref, q_ref, kf_ref, bf_ref, kb_ref, bb_ref,
                      v_ref, gate_ref, acc_ref, *, st, layer):
    d, nb = st.d, st.nb
    h = jnp.concatenate([_norm_mod(x_ref[r], gain_ref[...], m[:, d:2 * d], m[:, 0:d]).astype(BF16)
                         for r, m in enumerate(st.read_mod(modl_ref, modc_ref))], axis=0)
    lg = lbl_ref[...]
    e = jnp.exp(lg - jnp.max(lg, axis=0, keepdims=True))
    lb = jnp.sum(e[1:layer + 1], axis=0, keepdims=True) / jnp.sum(e, axis=0, keepdims=True)
    w = HG_HEADS * HG_DK

    def put(ref, cs, val):
        for r in range(nb):
            ref[r, :, cs] = val[r * TILE:(r + 1) * TILE]

    def finish(seg, c, acc):
        cs = slice(c * PROJ_COLS, (c + 1) * PROJ_COLS)
        if seg == 0:
            put(q_ref, cs, _silu(acc).astype(BF16))
        elif seg in (1, 2):
            k_ref, b_ref = (kf_ref, bf_ref) if seg == 1 else (kb_ref, bb_ref)
            f = lb[:, cs] + (1.0 - lb[:, cs]) * _sigmoid(acc)
            put(k_ref, cs, (1.0 - f).astype(BF16))
            g = jnp.log2(f)
            for r in range(nb):
                for j in range(TILE // CHUNK):
                    rows = slice(r * TILE + j * CHUNK, r * TILE + (j + 1) * CHUNK)
                    b_ref[r, j * CHUNK:(j + 1) * CHUNK, cs] = _chunk_cumsum(g[rows], reverse=seg == 2)
        elif seg == 3:
            put(v_ref, cs, acc.astype(BF16))
        else:
            put(gate_ref, cs, _silu(acc).astype(BF16))

    base = jnp.minimum(pl.program_id(0), 0)
    items = [(seg, c) for c in range(w // PROJ_COLS) for seg in (1, 2, 0, 4, 3)]
    slots = acc_ref.shape[0]
    for i in range(len(items) + PROJ_AHEAD):
        if i < len(items):
            seg, c = items[i]
            acc_ref[base + i % slots] = _dot(h, w_ref[:, seg * w + c * PROJ_COLS:seg * w + (c + 1) * PROJ_COLS])
        if i >= PROJ_AHEAD:
            finish(*items[i - PROJ_AHEAD], acc_ref[base + (i - PROJ_AHEAD) % slots])


def _hgrn_proj(st, xs, mod, gain, w, lb_logits, layer):
    b, t, d = xs.shape
    wd = HG_HEADS * HG_DK
    assert wd == d
    shapes = [(wd, BF16), (wd, BF16), (wd, F32), (wd, BF16), (wd, F32), (d, BF16), (d, BF16)]
    return pl.pallas_call(
        functools.partial(_hgrn_proj_kernel, st=st, layer=layer),
        grid=st.grid,
        in_specs=[st.tile(d), *st.mod(), st.row(d), _resident(w.shape), _resident(lb_logits.shape)],
        out_specs=[st.tile(n) for n, _ in shapes],
        out_shape=[jax.ShapeDtypeStruct((b, t, n), dt) for n, dt in shapes],
        scratch_shapes=[pltpu.VMEM((PROJ_AHEAD + 1, st.nb * TILE, PROJ_COLS), F32)],
        compiler_params=_params(("parallel", "parallel")),
        name="hgrn_proj",
    )(xs, mod, mod, gain, w, lb_logits)


def _level_table(reverse):
    t = np.arange(CHUNK)[:, None]
    s = np.arange(CHUNK)[None, :]
    x = t ^ s
    lvl = np.floor(np.log2(np.maximum(x, 1))).astype(np.int32)
    later = (t < s) if reverse else (t > s)
    lvl = np.where(later, lvl, -1)
    for i in range(CHUNK):
        j = i if reverse else ~i & (CHUNK - 1)
        lvl[i, i] = N_LEVELS if j == 0 else (j & -j).bit_length() - 1
    return np.concatenate([lvl, lvl], axis=1).astype(np.float32)


SLAB = 16


def _neg_abs16(x):
    sign = jnp.int16(-2 ** 15)
    return lax.bitcast_convert_type(lax.bitcast_convert_type(x, jnp.int16) | sign, BF16)


def _mid_rows(b2, half, reverse):
    n = b2.shape[-1]
    blk = 2 * half
    m = half if reverse else half - 1
    if half == 1:
        b3 = b2.reshape(CHUNK // SUBLANES, SUBLANES, n)
        odd = lax.broadcasted_iota(jnp.int32, b3.shape, 1) % 2 == 1
        if reverse:
            return jnp.where(odd, b3, pltpu.roll(b3, SUBLANES - 1, axis=1)).reshape(CHUNK, n)
        return jnp.where(odd, pltpu.roll(b3, 1, axis=1), b3).reshape(CHUNK, n)
    if blk >= SUBLANES:
        b3 = b2.reshape(CHUNK // blk, blk, n)
        return jnp.broadcast_to(b3[:, m:m + 1, :], b3.shape).reshape(CHUNK, n)
    b3 = b2.reshape(CHUNK // SUBLANES, SUBLANES, n)
    r = lax.broadcasted_iota(jnp.int32, b3.shape, 1)
    mid = jnp.broadcast_to(b3[:, m:m + 1, :], b3.shape)
    for j in range(1, SUBLANES // blk):
        mid = jnp.where(r >= j * blk, jnp.broadcast_to(b3[:, j * blk + m:j * blk + m + 1, :], b3.shape), mid)
    return mid.reshape(CHUNK, n)


def _scan_chunk(lvl_ref, q_ref, k_ref, b_ref, v_ref, o_ref, st_ref, reverse):
    @pl.when(pl.program_id(1) == 0)
    def _():
        st_ref[...] = jnp.zeros(st_ref.shape, F32)

    pw = 2 * HG_DK
    lvl = lvl_ref[...]
    lane = lax.broadcasted_iota(jnp.int32, (1, pw), 1)
    same_head = (lax.broadcasted_iota(jnp.int32, (pw, pw), 0) < HG_DK) == (lane < HG_DK)
    zero = jnp.zeros((), BF16)
    edge = 0 if reverse else CHUNK - 1
    slab = slice(0, SLAB) if reverse else slice(CHUNK - SLAB, CHUNK)

    def pair_rows(x):
        z = jnp.zeros((CHUNK, HG_DK), x.dtype)
        return jnp.concatenate([jnp.concatenate([x[:, :HG_DK], z], axis=1),
                                jnp.concatenate([z, x[:, HG_DK:]], axis=1)], axis=0)

    def pair_cols(xt):
        z = jnp.zeros((HG_DK, CHUNK), xt.dtype)
        return jnp.concatenate([jnp.concatenate([xt[:HG_DK], z], axis=1),
                                jnp.concatenate([z, xt[HG_DK:]], axis=1)], axis=0)

    for r, p in [(r, p) for r in range(q_ref.shape[0]) for p in range(HG_HEADS // 2)]:
        cs = slice(p * pw, (p + 1) * pw)
        q2 = q_ref[r, :, cs]
        k2 = k_ref[r, :, cs]
        b2 = b_ref[r, :, cs]
        v2 = v_ref[r, :, cs]
        b_edge = b2[edge:edge + 1, :]
        st = st_ref[r, p]
        o = _dot_nt(q2 * jnp.exp2(b2).astype(BF16), st.astype(BF16))
        qk = q2[slab].astype(F32) * k2[slab].astype(F32)
        diag = jnp.where(lane < CHUNK, jnp.sum(qk[:, :HG_DK], axis=-1, keepdims=True),
                         jnp.sum(qk[:, HG_DK:], axis=-1, keepdims=True))
        last = jnp.where(lvl[slab] == N_LEVELS, diag.astype(BF16), zero)
        rest = jnp.zeros((CHUNK - SLAB, pw), BF16)
        a = jnp.concatenate([last, rest] if reverse else [rest, last], axis=0)
        kt = k2.T
        for l in range(N_LEVELS):
            e = jnp.exp2(_neg_abs16((b2 - _mid_rows(b2, 2 ** l, reverse)).astype(BF16)))
            a_l = _dot(q2 * e, pair_cols(kt * e.T))
            a = jnp.where(lvl == l, a_l.astype(BF16), a)
        o = o + _dot(a, pair_rows(v2))
        o_ref[r, :, cs] = o.astype(o_ref.dtype)
        kdec = k2 * jnp.exp2(b_edge - b2).astype(BF16)
        upd = _dot_tn(v2, kdec)
        st_ref[r, p] = st * jnp.exp2(b_edge) + jnp.where(same_head, upd, 0.0)


def _scan_fwd_kernel(lvl_ref, q_ref, k_ref, b_ref, v_ref, o_ref, st_ref):
    _scan_chunk(lvl_ref, q_ref, k_ref, b_ref, v_ref, o_ref, st_ref, reverse=False)


def _scan_bwd_out_kernel(lvl_ref, q_ref, k_ref, b_ref, v_ref, of_ref, gate_ref, x_ref, modl_ref, modc_ref, og_ref,
                         w_ref, y_ref, st_ref, ob_ref, *, n_ctx, d):
    _scan_chunk(lvl_ref, q_ref, k_ref, b_ref, v_ref, ob_ref, st_ref, reverse=True)
    rows = q_ref.shape[0]
    og = og_ref[...]
    parts = []
    for r in range(rows):
        o = of_ref[r].astype(F32) + ob_ref[r]
        heads = []
        for hd in range(HG_HEADS):
            oh = o[:, hd * HG_DK:(hd + 1) * HG_DK]
            ms = jnp.mean(oh * oh, axis=-1, keepdims=True)
            heads.append(oh * lax.rsqrt(ms + EPS) * og)
        parts.append((jnp.concatenate(heads, axis=1) * gate_ref[r].astype(F32)).astype(BF16))
    acc = _dot(jnp.concatenate(parts, axis=0), w_ref[...])
    is_ctx = pl.program_id(1) < n_ctx
    for r in range(rows):
        g1 = jnp.where(is_ctx, modc_ref[...], modl_ref[r])[:, 2 * d:3 * d]
        y_ref[r] = x_ref[r] + g1 * acc[r * CHUNK:(r + 1) * CHUNK]


SCAN_ROWS = 4


def _scan_specs(b, w, seq, ctx, reverse):
    n_lat, n_ctx = seq // CHUNK, ctx // CHUNK
    n = n_lat + n_ctx

    def chunk(i):
        if reverse:
            return n - 1 - i
        return jnp.where(i < n_ctx, n_lat + i, i - n_ctx)

    rows = SCAN_ROWS if b % SCAN_ROWS == 0 else 1
    return rows, n, pl.BlockSpec((rows, CHUNK, w), lambda b, i: (b, chunk(i), 0))


def _hgrn_scan_fwd(lvl, q, k, bcum, v, seq, ctx):
    b, t, w = q.shape
    rows, n, spec = _scan_specs(b, w, seq, ctx, reverse=False)
    return pl.pallas_call(
        _scan_fwd_kernel,
        grid=(b // rows, n),
        in_specs=[_resident(lvl.shape), spec, spec, spec, spec],
        out_specs=spec,
        out_shape=jax.ShapeDtypeStruct((b, t, w), BF16),
        scratch_shapes=[pltpu.VMEM((rows, HG_HEADS // 2, 2 * HG_DK, 2 * HG_DK), F32)],
        compiler_params=_params(("parallel", "arbitrary")),
        name="hgrn_scan_fwd",
    )(lvl, q, k, bcum, v)


def _hgrn_scan_bwd_out(lvl, q, k, bcum, v, o_f, gate, xs, mod, o_gain, w_out, seq, ctx):
    b, t, w = q.shape
    d = w_out.shape[1]
    assert w == d
    rows, n, spec = _scan_specs(b, w, seq, ctx, reverse=True)
    return pl.pallas_call(
        functools.partial(_scan_bwd_out_kernel, n_ctx=ctx // CHUNK, d=d),
        grid=(b // rows, n),
        in_specs=[_resident(lvl.shape), spec, spec, spec, spec, spec, spec, spec,
                  pl.BlockSpec((rows, 1, 6 * d), lambda b, i: (b, 0, 0)),
                  pl.BlockSpec((None, 1, 6 * d), lambda b, i: (mod.shape[0] - 1, 0, 0)),
                  pl.BlockSpec((1, HG_DK), lambda b, i: (0, 0)), _resident(w_out.shape)],
        out_specs=spec,
        out_shape=jax.ShapeDtypeStruct((b, t, d), F32),
        scratch_shapes=[pltpu.VMEM((rows, HG_HEADS // 2, 2 * HG_DK, 2 * HG_DK), F32),
                        pltpu.VMEM((rows, CHUNK, w), F32)],
        compiler_params=_params(("parallel", "arbitrary")),
        name="hgrn_scan_bwd_out",
    )(lvl, q, k, bcum, v, o_f, gate, xs, mod, mod, o_gain, w_out)


FFN_COLS = 256
FFN_ROWS = 4


def _ffn_kernel(x_ref, prev_ref, next_ref, modl_ref, modc_ref, gain_ref, wu_ref, cw_ref, cb_ref, wd_ref, y_ref, u_ref,
                *, st):
    d, nb, n_lat, n_ctx = st.d, st.nb, st.n_lat, st.n_ctx
    t = pl.program_id(1)
    has_prev = jnp.logical_and(t != 0, t != n_lat).astype(F32)
    has_next = jnp.logical_and(t != n_lat - 1, t != n_lat + n_ctx - 1).astype(F32)
    keep = jnp.where(lax.broadcasted_iota(jnp.int32, (2 * SUBLANES, 1), 0) < SUBLANES, has_prev, has_next)
    gain = gain_ref[...]
    mods = st.read_mod(modl_ref, modc_ref)
    xs = [x_ref[r] for r in range(nb)]
    h = jnp.concatenate([_norm_mod(x, gain, m[:, 4 * d:5 * d], m[:, 3 * d:4 * d]).astype(BF16)
                         for x, m in zip(xs, mods)], axis=0)
    halos = [(_norm_mod(jnp.concatenate([prev_ref[r], next_ref[r]], axis=0), gain, m[:, 4 * d:5 * d],
                        m[:, 3 * d:4 * d]) * keep).astype(BF16) for r, m in enumerate(mods)]
    h_ext = jnp.concatenate([h] + halos, axis=0)
    f = wd_ref.shape[0]
    row = lax.broadcasted_iota(jnp.int32, (TILE, 1), 0)
    for c in range(f // FFN_COLS):
        cs = slice(c * FFN_COLS, (c + 1) * FFN_COLS)
        ge = _dot(h_ext, wu_ref[:, cs])
        val = _dot(h, wu_ref[:, f + c * FFN_COLS:f + (c + 1) * FFN_COLS])
        for r in range(nb):
            g = ge[r * TILE:(r + 1) * TILE]
            hb = nb * TILE + 2 * SUBLANES * r
            before = ge[hb + SUBLANES - 1:hb + SUBLANES]
            after = ge[hb + SUBLANES:hb + SUBLANES + 1]
            up = jnp.where(row == 0, before, pltpu.roll(g, 1, axis=0))
            dn = jnp.where(row == TILE - 1, after, pltpu.roll(g, TILE - 1, axis=0))
            conv = up * cw_ref[0:1, cs] + g * cw_ref[1:2, cs] + dn * cw_ref[2:3, cs] + cb_ref[:, cs]
            u_ref[r * TILE:(r + 1) * TILE, cs] = (_silu(conv) * val[r * TILE:(r + 1) * TILE]).astype(BF16)
    acc = _dot(u_ref[...], wd_ref[...])
    for r in range(nb):
        y_ref[r] = xs[r] + mods[r][:, 5 * d:6 * d] * acc[r * TILE:(r + 1) * TILE]


def _ffn(st, xs, mod, gain, w_up, conv_w, conv_b, w_down):
    d = st.d
    f = w_down.shape[0]
    per_tile = TILE // SUBLANES
    last = st.rows // SUBLANES - 1
    return pl.pallas_call(
        functools.partial(_ffn_kernel, st=st),
        grid=st.grid,
        in_specs=[st.tile(d),
                  pl.BlockSpec((st.nb, SUBLANES, d), lambda b, t: (b, jnp.maximum(t * per_tile - 1, 0), 0)),
                  pl.BlockSpec((st.nb, SUBLANES, d), lambda b, t: (b, jnp.minimum((t + 1) * per_tile, last), 0)),
                  *st.mod(), st.row(d), _resident(w_up.shape), _resident(conv_w.shape), _resident(conv_b.shape),
                  _resident(w_down.shape)],
        out_specs=st.tile(d),
        out_shape=jax.ShapeDtypeStruct((st.batch, st.rows, d), F32),
        scratch_shapes=[pltpu.VMEM((st.nb * TILE, f), BF16)],
        compiler_params=_params(("parallel", "parallel")),
        name="ffn",
    )(xs, xs, xs, mod, mod, gain, w_up, conv_w, conv_b, w_down)


def _rope_tables(seq, ctx):
    rows = seq // GRID_W
    row = jnp.repeat(jnp.arange(rows, dtype=F32), GRID_W)
    col = jnp.tile(jnp.arange(GRID_W, dtype=F32), rows)
    n_pairs = HEAD_DIM // 4
    inv = ROPE_THETA ** (-jnp.arange(n_pairs, dtype=F32) / n_pairs)
    ang_r, ang_c = row[:, None] * inv, col[:, None] * inv
    cos = jnp.concatenate([jnp.cos(ang_r)] * 2 + [jnp.cos(ang_c)] * 2, axis=1)
    sin = jnp.concatenate([-jnp.sin(ang_r), jnp.sin(ang_r), -jnp.sin(ang_c), jnp.sin(ang_c)], axis=1)
    cos = jnp.concatenate([cos, jnp.ones((ctx, HEAD_DIM), F32)], axis=0)
    sin = jnp.concatenate([sin, jnp.zeros((ctx, HEAD_DIM), F32)], axis=0)
    return jnp.tile(cos, (1, 4)), jnp.tile(sin, (1, 4))


def _head_block_constants():
    blk = 4 * HEAD_DIM
    i = np.arange(blk)
    gm = (i[:, None] // HEAD_DIM == i[None, :] // HEAD_DIM).astype(np.float32) / HEAD_DIM
    quarter = HEAD_DIM // 4
    partner = np.where((i % (2 * quarter)) < quarter, i + quarter, i - quarter)
    rot = (i[:, None] == partner[None, :]).astype(np.float32)
    return jnp.asarray(gm, BF16), jnp.asarray(rot, BF16)


def _query_head_order():
    return np.array([kh * GROUP + g for g in range(GROUP) for kh in range(N_KV_HEADS)])


def kernel(x, c, ctx, c_ctx, ada_w, ada_b, norm1_g, norm2_g, attn_w_in, attn_w_out, attn_q_gain, attn_k_gain,
           attn_sink, hgrn_w_in, hgrn_w_out, hgrn_o_gain, hgrn_lb_logits, ffn_w_up, ffn_conv_w, ffn_conv_b,
           ffn_w_down):
    batch, seq, d = x.shape
    n_ctx_tok = ctx.shape[1]
    depth = ada_w.shape[0]
    assert seq % TILE == 0 and n_ctx_tok % TILE == 0 and seq >= QBLK + 2 * WINDOW
    st = _Stream(batch, seq // TILE, n_ctx_tok // TILE, d)

    rows = -(-(batch + 1) // SUBLANES) * SUBLANES
    cc = jnp.zeros((rows, d), F32).at[:batch].set(c).at[batch].set(c_ctx)
    mod_all = _modulation(cc, ada_w, ada_b)[:, :batch + 1].reshape(depth, batch + 1, 1, 6 * d)

    cos, sin = _rope_tables(seq, n_ctx_tok)
    gm, rot = _head_block_constants()
    order = _query_head_order()
    nq = N_HEADS * HEAD_DIM
    lvl_f, lvl_b = jnp.asarray(_level_table(False), BF16), jnp.asarray(_level_table(True), BF16)

    st_ffn = _Stream(batch, seq // TILE, n_ctx_tok // TILE, d, FFN_ROWS)
    st_ffn_lat = _Stream(batch, seq // TILE, 0, d, FFN_ROWS)
    lat, con = x, ctx
    for layer in range(depth):
        j = layer // 2
        mod = mod_all[layer]
        g1 = norm1_g[layer].reshape(1, d)
        st_out = st_ffn_lat if layer == depth - 1 else st_ffn
        if layer % 2 == 0:
            w_in = attn_w_in[j]
            wq = w_in[:, :nq].reshape(d, N_HEADS, HEAD_DIM)[:, order].reshape(d, nq)
            w_in = jnp.concatenate([wq, w_in[:, nq:]], axis=1).astype(BF16)
            w_out = attn_w_out[j].reshape(N_HEADS, HEAD_DIM, d)[order].reshape(nq, d).astype(BF16)
            qg = jnp.tile(attn_q_gain[j], 4).reshape(1, 4 * HEAD_DIM)
            kg = jnp.tile(attn_k_gain[j], 4).reshape(1, 4 * HEAD_DIM)
            q, k, v = _attn_proj(st, lat, con, mod, g1, w_in, qg, kg, cos, sin, gm, rot)
            xs = _attention(attn_sink[j][order], lat, con, mod, q, k, v, w_out, seq, n_ctx_tok)
        else:
            q, kf, bf, kb, bb, v, gate = _hgrn_proj(st, lat, mod, g1, hgrn_w_in[j].astype(BF16), hgrn_lb_logits,
                                                    layer)
            o_f = _hgrn_scan_fwd(lvl_f, q, kf, bf, v, seq, n_ctx_tok)
            xs = _hgrn_scan_bwd_out(lvl_b, q, kb, bb, v, o_f, gate, lat, mod, hgrn_o_gain[j].reshape(1, HG_DK),
                                    hgrn_w_out[j].astype(BF16), seq, n_ctx_tok)
        xs = _ffn(st_out, xs, mod, norm2_g[layer].reshape(1, d), ffn_w_up[layer].astype(BF16), ffn_conv_w[layer],
                  ffn_conv_b[layer].reshape(1, -1), ffn_w_down[layer].astype(BF16))
        lat = con = xs
    return xs
```

```python
import functools
import math

import jax
import jax.numpy as jnp
import numpy as np
from jax import lax
from jax.experimental import pallas as pl
from jax.experimental.pallas import tpu as pltpu

F32 = jnp.float32
BF16 = jnp.bfloat16

EPS = 1e-6
HEAD_DIM = 64
N_HEADS = 16
N_KV_HEADS = 4
GROUP = N_HEADS // N_KV_HEADS
WINDOW = 128
ROPE_THETA = 10000.0
GRID_W = 64
HG_HEADS = 8
HG_DK = 128
CHUNK = 128
N_LEVELS = 7
TILE = 256
STREAM_ROWS = 2
QBLK = 128
ATTN_ROWS = 2
LANES = 128
SUBLANES = 8
NEG_BIG = -1e30
LOG2E = math.log2(math.e)
VMEM_LIMIT = 56 * 1024 * 1024


def _silu(x):
    return x / (1.0 + jnp.exp(-x))


def _sigmoid(x):
    return 1.0 / (1.0 + jnp.exp(-x))


def _dot(a, b):
    return jnp.dot(a, b, preferred_element_type=F32)


def _dot_nt(a, b):
    return lax.dot_general(a, b, (((1,), (1,)), ((), ())), preferred_element_type=F32)


def _dot_tn(a, b):
    return lax.dot_general(a, b, (((0,), (0,)), ((), ())), preferred_element_type=F32)


def _norm_mod(x, gain, scale, shift):
    ms = jnp.mean(x * x, axis=-1, keepdims=True)
    return (x * lax.rsqrt(ms + EPS)) * gain * (1.0 + scale) + shift


def _params(sem):
    return pltpu.CompilerParams(dimension_semantics=sem, vmem_limit_bytes=VMEM_LIMIT)


def _resident(shape):
    nd = len(shape)
    return pl.BlockSpec(shape, lambda *_: (0,) * nd, pipeline_mode=pl.Buffered(1))


MOD_COLS = 1536


def _mod_kernel(cc_ref, w_ref, b_ref, o_ref):
    a = _silu(cc_ref[...]).astype(BF16)
    o_ref[...] = _dot(a, w_ref[...].astype(BF16)) + b_ref[...]


def _modulation(cc, ada_w, ada_b):
    depth, d, n = ada_w.shape
    r = cc.shape[0]
    tn = MOD_COLS
    return pl.pallas_call(
        _mod_kernel,
        grid=(depth, n // tn),
        in_specs=[
            pl.BlockSpec((r, d), lambda l, j: (0, 0)),
            pl.BlockSpec((None, d, tn), lambda l, j: (l, 0, j)),
            pl.BlockSpec((None, 1, tn), lambda l, j: (l, 0, j)),
        ],
        out_specs=pl.BlockSpec((None, r, tn), lambda l, j: (l, 0, j)),
        out_shape=jax.ShapeDtypeStruct((depth, r, n), F32),
        compiler_params=_params(("parallel", "parallel")),
        name="modulation",
    )(cc, ada_w, ada_b.reshape(depth, 1, n))


class _Stream:
    def __init__(self, batch, n_lat, n_ctx, d, nb=STREAM_ROWS):
        self.batch, self.n_lat, self.n_ctx, self.d = batch, n_lat, n_ctx, d
        self.nb = nb if batch % nb == 0 else 1
        self.grid = (batch // self.nb, n_lat + n_ctx)
        self.rows = (n_lat + n_ctx) * TILE

    def tile(self, width, col=0):
        return pl.BlockSpec((self.nb, TILE, width), lambda b, t: (b, t, col))

    def sources(self, lat, ctx):
        n_lat = self.n_lat
        off = n_lat if ctx is lat else 0
        return [pl.BlockSpec((self.nb, TILE, self.d), lambda b, t: (b, jnp.minimum(t, n_lat - 1), 0)),
                pl.BlockSpec((self.nb, TILE, self.d), lambda b, t: (b, off + jnp.maximum(t - n_lat, 0), 0))]

    def is_ctx(self):
        return pl.program_id(1) >= self.n_lat

    def read(self, lat_ref, ctx_ref):
        return [jnp.where(self.is_ctx(), ctx_ref[r], lat_ref[r]) for r in range(self.nb)]

    def mod(self):
        return [pl.BlockSpec((self.nb, 1, 6 * self.d), lambda b, t: (b, 0, 0)),
                pl.BlockSpec((None, 1, 6 * self.d), lambda b, t: (self.batch, 0, 0))]

    def read_mod(self, lat_ref, ctx_ref):
        return [jnp.where(self.is_ctx(), ctx_ref[...], lat_ref[r]) for r in range(self.nb)]

    def row(self, width):
        return pl.BlockSpec((1, width), lambda b, t: (0, 0))


def _attn_proj_kernel(x_ref, c_ref, modl_ref, modc_ref, gain_ref, w_ref, qg_ref, kg_ref, cos_ref, sin_ref, gm_ref,
                      rot_ref, q_ref, k_ref, v_ref, *, st):
    d, nb = st.d, st.nb
    mods = st.read_mod(modl_ref, modc_ref)
    h = jnp.concatenate([_norm_mod(x, gain_ref[...], m[:, d:2 * d], m[:, 0:d]).astype(BF16)
                         for x, m in zip(st.read(x_ref, c_ref), mods)], axis=0)
    acc = _dot(h, w_ref[...])
    cos = jnp.concatenate([cos_ref[...]] * nb, axis=0)
    sin = jnp.concatenate([sin_ref[...]] * nb, axis=0)
    gm, rot = gm_ref[...], rot_ref[...]
    blk = 4 * HEAD_DIM
    n_qblk = N_HEADS * HEAD_DIM // blk

    def put(ref, cs, val):
        for r in range(nb):
            ref[r, :, cs] = val[r * TILE:(r + 1) * TILE]

    for c in range(n_qblk + 1):
        xc = acc[:, c * blk:(c + 1) * blk]
        ms = _dot((xc * xc).astype(BF16), gm)
        gain = qg_ref[...] if c < n_qblk else kg_ref[...]
        xn = xc * lax.rsqrt(ms + EPS) * gain
        y = xn * cos + _dot(xn.astype(BF16), rot) * sin
        if c < n_qblk:
            put(q_ref, slice(c * blk, (c + 1) * blk), (y * (HEAD_DIM ** -0.5 * LOG2E)).astype(BF16))
        else:
            put(k_ref, slice(None), y.astype(BF16))
    put(v_ref, slice(None), acc[:, (n_qblk + 1) * blk:].astype(BF16))


def _attn_proj(st, lat, ctx, mod, gain, w, qg, kg, cos, sin, gm, rot):
    b, d = st.batch, st.d
    t = st.rows
    nq, nk = N_HEADS * HEAD_DIM, N_KV_HEADS * HEAD_DIM
    return pl.pallas_call(
        functools.partial(_attn_proj_kernel, st=st),
        grid=st.grid,
        in_specs=[*st.sources(lat, ctx), *st.mod(), st.row(d), _resident(w.shape), st.row(nk), st.row(nk),
                  pl.BlockSpec((TILE, nk), lambda b, t: (t, 0)), pl.BlockSpec((TILE, nk), lambda b, t: (t, 0)),
                  _resident(gm.shape), _resident(rot.shape)],
        out_specs=[st.tile(nq), st.tile(nk), st.tile(nk)],
        out_shape=[jax.ShapeDtypeStruct((b, t, nq), BF16), jax.ShapeDtypeStruct((b, t, nk), BF16),
                   jax.ShapeDtypeStruct((b, t, nk), BF16)],
        compiler_params=_params(("parallel", "parallel")),
        name="attn_proj",
    )(lat, ctx, mod, mod, gain, w, qg, kg, cos, sin, gm, rot)


def _attn_kernel(sink_ref, x_ref, c_ref, modl_ref, modc_ref, q_ref, k_ref, v_ref, w_ref, y_ref, o_ref,
                 *, seq, ctx, d):
    n = pl.program_id(1)
    n_lat = seq // QBLK
    wlen = QBLK + 2 * WINDOW
    is_ctx = n >= n_lat
    ws = pl.multiple_of(jnp.clip(n * QBLK - WINDOW, 0, seq - wlen), QBLK)
    nkeys = wlen + ctx
    qpos = n * QBLK + lax.broadcasted_iota(jnp.int32, (QBLK, wlen), 0)
    kpos = ws + lax.broadcasted_iota(jnp.int32, (QBLK, wlen), 1)
    in_window = (jnp.abs(qpos - kpos) <= WINDOW) & jnp.logical_not(is_ctx)
    bias = jnp.where(in_window, 0.0, NEG_BIG).astype(F32)
    left = lax.broadcasted_iota(jnp.int32, (1, LANES), 1) < HEAD_DIM
    kv_of_lane = lax.broadcasted_iota(jnp.int32, (1, N_KV_HEADS * HEAD_DIM), 1) // HEAD_DIM
    zero = jnp.zeros((), BF16)
    blk = N_KV_HEADS * HEAD_DIM
    nb = q_ref.shape[0]
    for r, g in [(r, g) for r in range(nb) for g in range(GROUP)]:
        if g == 0:
            kall = jnp.concatenate([k_ref[r, pl.ds(ws, wlen), :], k_ref[r, seq:seq + ctx, :]], axis=0)
            vall = jnp.concatenate([v_ref[r, pl.ds(ws, wlen), :], v_ref[r, seq:seq + ctx, :]], axis=0)
            kbd = [jnp.concatenate([jnp.where(left, kall[:, p * LANES:(p + 1) * LANES], zero),
                                    jnp.where(left, zero, kall[:, p * LANES:(p + 1) * LANES])], axis=0)
                   for p in range(N_KV_HEADS // 2)]
            vbd = jnp.concatenate([jnp.where(kv_of_lane == kh, vall, zero) for kh in range(N_KV_HEADS)], axis=0)
        probs, invs = [], []
        for p in range(N_KV_HEADS // 2):
            q2 = q_ref[r, :, g * blk + p * LANES:g * blk + (p + 1) * LANES]
            s = _dot_nt(q2, kbd[p])
            for half in range(2):
                sink = sink_ref[g * N_KV_HEADS + 2 * p + half] * LOG2E
                sw = s[:, half * nkeys:half * nkeys + wlen] + bias
                sc = s[:, half * nkeys + wlen:(half + 1) * nkeys]
                m = jnp.maximum(jnp.maximum(jnp.max(sw, axis=-1, keepdims=True),
                                            jnp.max(sc, axis=-1, keepdims=True)), sink)
                pw, pc = jnp.exp2(sw - m), jnp.exp2(sc - m)
                den = jnp.sum(pw, axis=-1, keepdims=True) + jnp.sum(pc, axis=-1, keepdims=True) + jnp.exp2(sink - m)
                probs += [pw.astype(BF16), pc.astype(BF16)]
                invs.append(1.0 / den)
        o = _dot(jnp.concatenate(probs, axis=1), vbd)
        inv = jnp.where(kv_of_lane == 0, invs[0], jnp.where(kv_of_lane == 1, invs[1],
                                                            jnp.where(kv_of_lane == 2, invs[2], invs[3])))
        o_ref[r * QBLK:(r + 1) * QBLK, g * blk:(g + 1) * blk] = (o * inv).astype(BF16)
    acc = _dot(o_ref[...], w_ref[...])
    for r in range(nb):
        x = jnp.where(is_ctx, c_ref[r], x_ref[r])
        g1 = jnp.where(is_ctx, modc_ref[...], modl_ref[r])[:, 2 * d:3 * d]
        y_ref[r] = x + g1 * acc[r * QBLK:(r + 1) * QBLK]


def _attention(sink, lat, con, mod, q, k, v, w_out, seq, ctx):
    b, t, nq = q.shape
    nk = k.shape[-1]
    d = w_out.shape[1]
    n_lat = seq // QBLK
    off = n_lat if con is lat else 0
    nb = ATTN_ROWS if b % ATTN_ROWS == 0 else 1
    return pl.pallas_call(
        functools.partial(_attn_kernel, seq=seq, ctx=ctx, d=d),
        grid=(b // nb, t // QBLK),
        in_specs=[pl.BlockSpec(memory_space=pltpu.SMEM),
                  pl.BlockSpec((nb, QBLK, d), lambda b, n: (b, jnp.minimum(n, n_lat - 1), 0)),
                  pl.BlockSpec((nb, QBLK, d), lambda b, n: (b, off + jnp.maximum(n - n_lat, 0), 0)),
                  pl.BlockSpec((nb, 1, 6 * d), lambda b, n: (b, 0, 0)),
                  pl.BlockSpec((None, 1, 6 * d), lambda b, n: (mod.shape[0] - 1, 0, 0)),
                  pl.BlockSpec((nb, QBLK, nq), lambda b, n: (b, n, 0)),
                  pl.BlockSpec((nb, t, nk), lambda b, n: (b, 0, 0)),
                  pl.BlockSpec((nb, t, nk), lambda b, n: (b, 0, 0)),
                  _resident(w_out.shape)],
        out_specs=pl.BlockSpec((nb, QBLK, d), lambda b, n: (b, n, 0)),
        out_shape=jax.ShapeDtypeStruct((b, t, d), F32),
        scratch_shapes=[pltpu.VMEM((nb * QBLK, nq), BF16)],
        compiler_params=_params(("parallel", "parallel")),
        name="attention",
    )(sink, lat, con, mod, mod, q, k, v, w_out)


PROJ_COLS = 256
PROJ_AHEAD = 2


def _chunk_cumsum(g, reverse):
    n = g.shape[-1]
    groups = CHUNK // SUBLANES
    g = g.reshape(groups, SUBLANES, n)
    r = lax.broadcasted_iota(jnp.int32, g.shape, 1)
    d = 1
    while d < SUBLANES:
        if reverse:
            g = g + jnp.where(r < SUBLANES - d, pltpu.roll(g, SUBLANES - d, axis=1), 0.0)
        else:
            g = g + jnp.where(r >= d, pltpu.roll(g, d, axis=1), 0.0)
        d *= 2
    g = g.reshape(CHUNK, n)
    edge = 0 if reverse else SUBLANES - 1
    parts = [None] * groups
    carry = None
    for j in (reversed(range(groups)) if reverse else range(groups)):
        blk = g[j * SUBLANES:(j + 1) * SUBLANES]
        if carry is not None:
            blk = blk + carry
        parts[j] = blk
        carry = blk[edge:edge + 1]
    return jnp.concatenate(parts, axis=0)


def _hgrn_proj_kernel(x_ref, modl_ref, modc_ref, gain_ref, w_ref, lbl_ref, q_ref, kf_ref, bf_ref, kb_ref, bb_ref,
                      v_ref, gate_ref, acc_ref, *, st, layer):
    d, nb = st.d, st.nb
    h = jnp.concatenate([_norm_mod(x_ref[r], gain_ref[...], m[:, d:2 * d], m[:, 0:d]).astype(BF16)
                         for r, m in enumerate(st.read_mod(modl_ref, modc_ref))], axis=0)
    lg = lbl_ref[...]
    e = jnp.exp(lg - jnp.max(lg, axis=0, keepdims=True))
    lb = jnp.sum(e[1:layer + 1], axis=0, keepdims=True) / jnp.sum(e, axis=0, keepdims=True)
    w = HG_HEADS * HG_DK

    def put(ref, cs, val):
        for r in range(nb):
            ref[r, :, cs] = val[r * TILE:(r + 1) * TILE]

    def finish(seg, c, acc):
        cs = slice(c * PROJ_COLS, (c + 1) * PROJ_COLS)
        if seg == 0:
            put(q_ref, cs, _silu(acc).astype(BF16))
        elif seg in (1, 2):
            k_ref, b_ref = (kf_ref, bf_ref) if seg == 1 else (kb_ref, bb_ref)
            f = lb[:, cs] + (1.0 - lb[:, cs]) * _sigmoid(acc)
            put(k_ref, cs, (1.0 - f).astype(BF16))
            g = jnp.log2(f)
            for r in range(nb):
                for j in range(TILE // CHUNK):
                    rows = slice(r * TILE + j * CHUNK, r * TILE + (j + 1) * CHUNK)
                    b_ref[r, j * CHUNK:(j + 1) * CHUNK, cs] = _chunk_cumsum(g[rows], reverse=seg == 2)
        elif seg == 3:
            put(v_ref, cs, acc.astype(BF16))
        else:
            put(gate_ref, cs, _silu(acc).astype(BF16))

    base = jnp.minimum(pl.program_id(0), 0)
    items = [(seg, c) for c in range(w // PROJ_COLS) for seg in (1, 2, 0, 4, 3)]
    slots = acc_ref.shape[0]
    for i in range(len(items) + PROJ_AHEAD):
        if i < len(items):
            seg, c = items[i]
            acc_ref[base + i % slots] = _dot(h, w_ref[:, seg * w + c * PROJ_COLS:seg * w + (c + 1) * PROJ_COLS])
        if i >= PROJ_AHEAD:
            finish(*items[i - PROJ_AHEAD], acc_ref[base + (i - PROJ_AHEAD) % slots])


def _hgrn_proj(st, xs, mod, gain, w, lb_logits, layer):
    b, t, d = xs.shape
    wd = HG_HEADS * HG_DK
    assert wd == d
    shapes = [(wd, BF16), (wd, BF16), (wd, F32), (wd, BF16), (wd, F32), (d, BF16), (d, BF16)]
    return pl.pallas_call(
        functools.partial(_hgrn_proj_kernel, st=st, layer=layer),
        grid=st.grid,
        in_specs=[st.tile(d), *st.mod(), st.row(d), _resident(w.shape), _resident(lb_logits.shape)],
        out_specs=[st.tile(n) for n, _ in shapes],
        out_shape=[jax.ShapeDtypeStruct((b, t, n), dt) for n, dt in shapes],
        scratch_shapes=[pltpu.VMEM((PROJ_AHEAD + 1, st.nb * TILE, PROJ_COLS), F32)],
        compiler_params=_params(("parallel", "parallel")),
        name="hgrn_proj",
    )(xs, mod, mod, gain, w, lb_logits)


def _level_table(reverse):
    t = np.arange(CHUNK)[:, None]
    s = np.arange(CHUNK)[None, :]
    x = t ^ s
    lvl = np.floor(np.log2(np.maximum(x, 1))).astype(np.int32)
    later = (t < s) if reverse else (t > s)
    lvl = np.where(later, lvl, -1)
    for i in range(CHUNK):
        j = i if reverse else ~i & (CHUNK - 1)
        lvl[i, i] = N_LEVELS if j == 0 else (j & -j).bit_length() - 1
    return np.concatenate([lvl, lvl], axis=1).astype(np.float32)


SLAB = 16


def _neg_abs16(x):
    sign = jnp.int16(-2 ** 15)
    return lax.bitcast_convert_type(lax.bitcast_convert_type(x, jnp.int16) | sign, BF16)


def _mid_rows(b2, half, reverse):
    n = b2.shape[-1]
    blk = 2 * half
    m = half if reverse else half - 1
    if half == 1:
        b3 = b2.reshape(CHUNK // SUBLANES, SUBLANES, n)
        odd = lax.broadcasted_iota(jnp.int32, b3.shape, 1) % 2 == 1
        if reverse:
            return jnp.where(odd, b3, pltpu.roll(b3, SUBLANES - 1, axis=1)).reshape(CHUNK, n)
        return jnp.where(odd, pltpu.roll(b3, 1, axis=1), b3).reshape(CHUNK, n)
    if blk >= SUBLANES:
        b3 = b2.reshape(CHUNK // blk, blk, n)
        return jnp.broadcast_to(b3[:, m:m + 1, :], b3.shape).reshape(CHUNK, n)
    b3 = b2.reshape(CHUNK // SUBLANES, SUBLANES, n)
    r = lax.broadcasted_iota(jnp.int32, b3.shape, 1)
    mid = jnp.broadcast_to(b3[:, m:m + 1, :], b3.shape)
    for j in range(1, SUBLANES // blk):
        mid = jnp.where(r >= j * blk, jnp.broadcast_to(b3[:, j * blk + m:j * blk + m + 1, :], b3.shape), mid)
    return mid.reshape(CHUNK, n)


def _scan_chunk(lvl_ref, q_ref, k_ref, b_ref, v_ref, o_ref, st_ref, reverse):
    @pl.when(pl.program_id(1) == 0)
    def _():
        st_ref[...] = jnp.zeros(st_ref.shape, F32)

    pw = 2 * HG_DK
    lvl = lvl_ref[...]
    lane = lax.broadcasted_iota(jnp.int32, (1, pw), 1)
    same_head = (lax.broadcasted_iota(jnp.int32, (pw, pw), 0) < HG_DK) == (lane < HG_DK)
    zero = jnp.zeros((), BF16)
    edge = 0 if reverse else CHUNK - 1
    slab = slice(0, SLAB) if reverse else slice(CHUNK - SLAB, CHUNK)

    def pair_rows(x):
        z = jnp.zeros((CHUNK, HG_DK), x.dtype)
        return jnp.concatenate([jnp.concatenate([x[:, :HG_DK], z], axis=1),
                                jnp.concatenate([z, x[:, HG_DK:]], axis=1)], axis=0)

    def pair_cols(xt):
        z = jnp.zeros((HG_DK, CHUNK), xt.dtype)
        return jnp.concatenate([jnp.concatenate([xt[:HG_DK], z], axis=1),
                                jnp.concatenate([z, xt[HG_DK:]], axis=1)], axis=0)

    for r, p in [(r, p) for r in range(q_ref.shape[0]) for p in range(HG_HEADS // 2)]:
        cs = slice(p * pw, (p + 1) * pw)
        q2 = q_ref[r, :, cs]
        k2 = k_ref[r, :, cs]
        b2 = b_ref[r, :, cs]
        v2 = v_ref[r, :, cs]
        b_edge = b2[edge:edge + 1, :]
        st = st_ref[r, p]
        o = _dot_nt(q2 * jnp.exp2(b2).astype(BF16), st.astype(BF16))
        qk = q2[slab].astype(F32) * k2[slab].astype(F32)
        diag = jnp.where(lane < CHUNK, jnp.sum(qk[:, :HG_DK], axis=-1, keepdims=True),
                         jnp.sum(qk[:, HG_DK:], axis=-1, keepdims=True))
        last = jnp.where(lvl[slab] == N_LEVELS, diag.astype(BF16), zero)
        rest = jnp.zeros((CHUNK - SLAB, pw), BF16)
        a = jnp.concatenate([last, rest] if reverse else [rest, last], axis=0)
        kt = k2.T
        for l in range(N_LEVELS):
            e = jnp.exp2(_neg_abs16((b2 - _mid_rows(b2, 2 ** l, reverse)).astype(BF16)))
            a_l = _dot(q2 * e, pair_cols(kt * e.T))
            a = jnp.where(lvl == l, a_l.astype(BF16), a)
        o = o + _dot(a, pair_rows(v2))
        o_ref[r, :, cs] = o.astype(o_ref.dtype)
        kdec = k2 * jnp.exp2(b_edge - b2).astype(BF16)
        upd = _dot_tn(v2, kdec)
        st_ref[r, p] = st * jnp.exp2(b_edge) + jnp.where(same_head, upd, 0.0)


def _scan_fwd_kernel(lvl_ref, q_ref, k_ref, b_ref, v_ref, o_ref, st_ref):
    _scan_chunk(lvl_ref, q_ref, k_ref, b_ref, v_ref, o_ref, st_ref, reverse=False)


def _scan_bwd_out_kernel(lvl_ref, q_ref, k_ref, b_ref, v_ref, of_ref, gate_ref, x_ref, modl_ref, modc_ref, og_ref,
                         w_ref, y_ref, st_ref, ob_ref, *, n_ctx, d):
    _scan_chunk(lvl_ref, q_ref, k_ref, b_ref, v_ref, ob_ref, st_ref, reverse=True)
    rows = q_ref.shape[0]
    og = og_ref[...]
    parts = []
    for r in range(rows):
        o = of_ref[r].astype(F32) + ob_ref[r]
        heads = []
        for hd in range(HG_HEADS):
            oh = o[:, hd * HG_DK:(hd + 1) * HG_DK]
            ms = jnp.mean(oh * oh, axis=-1, keepdims=True)
            heads.append(oh * lax.rsqrt(ms + EPS) * og)
        parts.append((jnp.concatenate(heads, axis=1) * gate_ref[r].astype(F32)).astype(BF16))
    acc = _dot(jnp.concatenate(parts, axis=0), w_ref[...])
    is_ctx = pl.program_id(1) < n_ctx
    for r in range(rows):
        g1 = jnp.where(is_ctx, modc_ref[...], modl_ref[r])[:, 2 * d:3 * d]
        y_ref[r] = x_ref[r] + g1 * acc[r * CHUNK:(r + 1) * CHUNK]


SCAN_ROWS_FWD = 8
SCAN_ROWS_BWD = 4


def _scan_specs(b, w, seq, ctx, reverse):
    n_lat, n_ctx = seq // CHUNK, ctx // CHUNK
    n = n_lat + n_ctx

    def chunk(i):
        if reverse:
            return n - 1 - i
        return jnp.where(i < n_ctx, n_lat + i, i - n_ctx)

    rows = SCAN_ROWS_BWD if reverse else SCAN_ROWS_FWD
    rows = rows if b % rows == 0 else 1
    return rows, n, pl.BlockSpec((rows, CHUNK, w), lambda b, i: (b, chunk(i), 0))


def _hgrn_scan_fwd(lvl, q, k, bcum, v, seq, ctx):
    b, t, w = q.shape
    rows, n, spec = _scan_specs(b, w, seq, ctx, reverse=False)
    return pl.pallas_call(
        _scan_fwd_kernel,
        grid=(b // rows, n),
        in_specs=[_resident(lvl.shape), spec, spec, spec, spec],
        out_specs=spec,
        out_shape=jax.ShapeDtypeStruct((b, t, w), BF16),
        scratch_shapes=[pltpu.VMEM((rows, HG_HEADS // 2, 2 * HG_DK, 2 * HG_DK), F32)],
        compiler_params=_params(("parallel", "arbitrary")),
        name="hgrn_scan_fwd",
    )(lvl, q, k, bcum, v)


def _hgrn_scan_bwd_out(lvl, q, k, bcum, v, o_f, gate, xs, mod, o_gain, w_out, seq, ctx):
    b, t, w = q.shape
    d = w_out.shape[1]
    assert w == d
    rows, n, spec = _scan_specs(b, w, seq, ctx, reverse=True)
    return pl.pallas_call(
        functools.partial(_scan_bwd_out_kernel, n_ctx=ctx // CHUNK, d=d),
        grid=(b // rows, n),
        in_specs=[_resident(lvl.shape), spec, spec, spec, spec, spec, spec, spec,
                  pl.BlockSpec((rows, 1, 6 * d), lambda b, i: (b, 0, 0)),
                  pl.BlockSpec((None, 1, 6 * d), lambda b, i: (mod.shape[0] - 1, 0, 0)),
                  pl.BlockSpec((1, HG_DK), lambda b, i: (0, 0)), _resident(w_out.shape)],
        out_specs=spec,
        out_shape=jax.ShapeDtypeStruct((b, t, d), F32),
        scratch_shapes=[pltpu.VMEM((rows, HG_HEADS // 2, 2 * HG_DK, 2 * HG_DK), F32),
                        pltpu.VMEM((rows, CHUNK, w), F32)],
        compiler_params=_params(("parallel", "arbitrary")),
        name="hgrn_scan_bwd_out",
    )(lvl, q, k, bcum, v, o_f, gate, xs, mod, mod, o_gain, w_out)


FFN_COLS = 256
FFN_ROWS = 4


def _ffn_kernel(x_ref, prev_ref, next_ref, modl_ref, modc_ref, gain_ref, wu_ref, cw_ref, cb_ref, wd_ref, y_ref, u_ref,
                *, st):
    d, nb, n_lat, n_ctx = st.d, st.nb, st.n_lat, st.n_ctx
    t = pl.program_id(1)
    has_prev = jnp.logical_and(t != 0, t != n_lat).astype(F32)
    has_next = jnp.logical_and(t != n_lat - 1, t != n_lat + n_ctx - 1).astype(F32)
    keep = jnp.where(lax.broadcasted_iota(jnp.int32, (2 * SUBLANES, 1), 0) < SUBLANES, has_prev, has_next)
    gain = gain_ref[...]
    mods = st.read_mod(modl_ref, modc_ref)
    xs = [x_ref[r] for r in range(nb)]
    h = jnp.concatenate([_norm_mod(x, gain, m[:, 4 * d:5 * d], m[:, 3 * d:4 * d]).astype(BF16)
                         for x, m in zip(xs, mods)], axis=0)
    halos = [(_norm_mod(jnp.concatenate([prev_ref[r], next_ref[r]], axis=0), gain, m[:, 4 * d:5 * d],
                        m[:, 3 * d:4 * d]) * keep).astype(BF16) for r, m in enumerate(mods)]
    h_ext = jnp.concatenate([h] + halos, axis=0)
    f = wd_ref.shape[0]
    row = lax.broadcasted_iota(jnp.int32, (TILE, 1), 0)
    for c in range(f // FFN_COLS):
        cs = slice(c * FFN_COLS, (c + 1) * FFN_COLS)
        ge = _dot(h_ext, wu_ref[:, cs])
        val = _dot(h, wu_ref[:, f + c * FFN_COLS:f + (c + 1) * FFN_COLS])
        for r in range(nb):
            g = ge[r * TILE:(r + 1) * TILE]
            hb = nb * TILE + 2 * SUBLANES * r
            before = ge[hb + SUBLANES - 1:hb + SUBLANES]
            after = ge[hb + SUBLANES:hb + SUBLANES + 1]
            up = jnp.where(row == 0, before, pltpu.roll(g, 1, axis=0))
            dn = jnp.where(row == TILE - 1, after, pltpu.roll(g, TILE - 1, axis=0))
            conv = up * cw_ref[0:1, cs] + g * cw_ref[1:2, cs] + dn * cw_ref[2:3, cs] + cb_ref[:, cs]
            u_ref[r * TILE:(r + 1) * TILE, cs] = (_silu(conv) * val[r * TILE:(r + 1) * TILE]).astype(BF16)
    acc = _dot(u_ref[...], wd_ref[...])
    for r in range(nb):
        y_ref[r] = xs[r] + mods[r][:, 5 * d:6 * d] * acc[r * TILE:(r + 1) * TILE]


def _ffn(st, xs, mod, gain, w_up, conv_w, conv_b, w_down):
    d = st.d
    f = w_down.shape[0]
    per_tile = TILE // SUBLANES
    last = st.rows // SUBLANES - 1
    return pl.pallas_call(
        functools.partial(_ffn_kernel, st=st),
        grid=st.grid,
        in_specs=[st.tile(d),
                  pl.BlockSpec((st.nb, SUBLANES, d), lambda b, t: (b, jnp.maximum(t * per_tile - 1, 0), 0)),
                  pl.BlockSpec((st.nb, SUBLANES, d), lambda b, t: (b, jnp.minimum((t + 1) * per_tile, last), 0)),
                  *st.mod(), st.row(d), _resident(w_up.shape), _resident(conv_w.shape), _resident(conv_b.shape),
                  _resident(w_down.shape)],
        out_specs=st.tile(d),
        out_shape=jax.ShapeDtypeStruct((st.batch, st.rows, d), F32),
        scratch_shapes=[pltpu.VMEM((st.nb * TILE, f), BF16)],
        compiler_params=_params(("parallel", "parallel")),
        name="ffn",
    )(xs, xs, xs, mod, mod, gain, w_up, conv_w, conv_b, w_down)


def _rope_tables(seq, ctx):
    rows = seq // GRID_W
    row = jnp.repeat(jnp.arange(rows, dtype=F32), GRID_W)
    col = jnp.tile(jnp.arange(GRID_W, dtype=F32), rows)
    n_pairs = HEAD_DIM // 4
    inv = ROPE_THETA ** (-jnp.arange(n_pairs, dtype=F32) / n_pairs)
    ang_r, ang_c = row[:, None] * inv, col[:, None] * inv
    cos = jnp.concatenate([jnp.cos(ang_r)] * 2 + [jnp.cos(ang_c)] * 2, axis=1)
    sin = jnp.concatenate([-jnp.sin(ang_r), jnp.sin(ang_r), -jnp.sin(ang_c), jnp.sin(ang_c)], axis=1)
    cos = jnp.concatenate([cos, jnp.ones((ctx, HEAD_DIM), F32)], axis=0)
    sin = jnp.concatenate([sin, jnp.zeros((ctx, HEAD_DIM), F32)], axis=0)
    return jnp.tile(cos, (1, 4)), jnp.tile(sin, (1, 4))


def _head_block_constants():
    blk = 4 * HEAD_DIM
    i = np.arange(blk)
    gm = (i[:, None] // HEAD_DIM == i[None, :] // HEAD_DIM).astype(np.float32) / HEAD_DIM
    quarter = HEAD_DIM // 4
    partner = np.where((i % (2 * quarter)) < quarter, i + quarter, i - quarter)
    rot = (i[:, None] == partner[None, :]).astype(np.float32)
    return jnp.asarray(gm, BF16), jnp.asarray(rot, BF16)


def _query_head_order():
    return np.array([kh * GROUP + g for g in range(GROUP) for kh in range(N_KV_HEADS)])


def kernel(x, c, ctx, c_ctx, ada_w, ada_b, norm1_g, norm2_g, attn_w_in, attn_w_out, attn_q_gain, attn_k_gain,
           attn_sink, hgrn_w_in, hgrn_w_out, hgrn_o_gain, hgrn_lb_logits, ffn_w_up, ffn_conv_w, ffn_conv_b,
           ffn_w_down):
    batch, seq, d = x.shape
    n_ctx_tok = ctx.shape[1]
    depth = ada_w.shape[0]
    assert seq % TILE == 0 and n_ctx_tok % TILE == 0 and seq >= QBLK + 2 * WINDOW
    st = _Stream(batch, seq // TILE, n_ctx_tok // TILE, d)

    rows = -(-(batch + 1) // SUBLANES) * SUBLANES
    cc = jnp.zeros((rows, d), F32).at[:batch].set(c).at[batch].set(c_ctx)
    mod_all = _modulation(cc, ada_w, ada_b)[:, :batch + 1].reshape(depth, batch + 1, 1, 6 * d)

    cos, sin = _rope_tables(seq, n_ctx_tok)
    gm, rot = _head_block_constants()
    order = _query_head_order()
    nq = N_HEADS * HEAD_DIM
    lvl_f, lvl_b = jnp.asarray(_level_table(False), BF16), jnp.asarray(_level_table(True), BF16)

    st_ffn = _Stream(batch, seq // TILE, n_ctx_tok // TILE, d, FFN_ROWS)
    st_ffn_lat = _Stream(batch, seq // TILE, 0, d, FFN_ROWS)
    lat, con = x, ctx
    for layer in range(depth):
        j = layer // 2
        mod = mod_all[layer]
        g1 = norm1_g[layer].reshape(1, d)
        st_out = st_ffn_lat if layer == depth - 1 else st_ffn
        if layer % 2 == 0:
            w_in = attn_w_in[j]
            wq = w_in[:, :nq].reshape(d, N_HEADS, HEAD_DIM)[:, order].reshape(d, nq)
            w_in = jnp.concatenate([wq, w_in[:, nq:]], axis=1).astype(BF16)
            w_out = attn_w_out[j].reshape(N_HEADS, HEAD_DIM, d)[order].reshape(nq, d).astype(BF16)
            qg = jnp.tile(attn_q_gain[j], 4).reshape(1, 4 * HEAD_DIM)
            kg = jnp.tile(attn_k_gain[j], 4).reshape(1, 4 * HEAD_DIM)
            q, k, v = _attn_proj(st, lat, con, mod, g1, w_in, qg, kg, cos, sin, gm, rot)
            xs = _attention(attn_sink[j][order], lat, con, mod, q, k, v, w_out, seq, n_ctx_tok)
        else:
            q, kf, bf, kb, bb, v, gate = _hgrn_proj(st, lat, mod, g1, hgrn_w_in[j].astype(BF16), hgrn_lb_logits,
                                                    layer)
            o_f = _hgrn_scan_fwd(lvl_f, q, kf, bf, v, seq, n_ctx_tok)
            xs = _hgrn_scan_bwd_out(lvl_b, q, kb, bb, v, o_f, gate, lat, mod, hgrn_o_gain[j].reshape(1, HG_DK),
                                    hgrn_w_out[j].astype(BF16), seq, n_ctx_tok)
        xs = _ffn(st_out, xs, mod, norm2_g[layer].reshape(1, d), ffn_w_up[layer].astype(BF16), ffn_conv_w[layer],
                  ffn_conv_b[layer].reshape(1, -1), ffn_w_down[layer].astype(BF16))
        lat = con = xs
    return xs
```

```python
import functools
import math

import jax
import jax.numpy as jnp
import numpy as np
from jax import lax
from jax.experimental import pallas as pl
from jax.experimental.pallas import tpu as pltpu

F32 = jnp.float32
BF16 = jnp.bfloat16

EPS = 1e-6
HEAD_DIM = 64
N_HEADS = 16
N_KV_HEADS = 4
GROUP = N_HEADS // N_KV_HEADS
WINDOW = 128
ROPE_THETA = 10000.0
GRID_W = 64
HG_HEADS = 8
HG_DK = 128
CHUNK = 128
N_LEVELS = 7
TILE = 256
STREAM_ROWS = 2
WIDE_ROWS = 4
QBLK = 128
ATTN_ROWS = 2
LANES = 128
SUBLANES = 8
NEG_BIG = -1e30
LOG2E = math.log2(math.e)
VMEM_LIMIT = 56 * 1024 * 1024


def _silu(x):
    return x / (1.0 + jnp.exp(-x))


def _sigmoid(x):
    return 1.0 / (1.0 + jnp.exp(-x))


def _dot(a, b):
    return jnp.dot(a, b, preferred_element_type=F32)


def _dot_nt(a, b):
    return lax.dot_general(a, b, (((1,), (1,)), ((), ())), preferred_element_type=F32)


def _dot_tn(a, b):
    return lax.dot_general(a, b, (((0,), (0,)), ((), ())), preferred_element_type=F32)


def _norm_mod(x, gain, scale, shift):
    ms = jnp.mean(x * x, axis=-1, keepdims=True)
    return (x * lax.rsqrt(ms + EPS)) * gain * (1.0 + scale) + shift


def _params(sem):
    return pltpu.CompilerParams(dimension_semantics=sem, vmem_limit_bytes=VMEM_LIMIT)


def _resident(shape):
    nd = len(shape)
    return pl.BlockSpec(shape, lambda *_: (0,) * nd, pipeline_mode=pl.Buffered(1))


MOD_COLS = 1536


def _mod_kernel(cc_ref, w_ref, b_ref, o_ref):
    a = _silu(cc_ref[...]).astype(BF16)
    o_ref[...] = _dot(a, w_ref[...].astype(BF16)) + b_ref[...]


def _modulation(cc, ada_w, ada_b):
    depth, d, n = ada_w.shape
    r = cc.shape[0]
    tn = MOD_COLS
    return pl.pallas_call(
        _mod_kernel,
        grid=(depth, n // tn),
        in_specs=[
            pl.BlockSpec((r, d), lambda l, j: (0, 0)),
            pl.BlockSpec((None, d, tn), lambda l, j: (l, 0, j)),
            pl.BlockSpec((None, 1, tn), lambda l, j: (l, 0, j)),
        ],
        out_specs=pl.BlockSpec((None, r, tn), lambda l, j: (l, 0, j)),
        out_shape=jax.ShapeDtypeStruct((depth, r, n), F32),
        compiler_params=_params(("parallel", "parallel")),
        name="modulation",
    )(cc, ada_w, ada_b.reshape(depth, 1, n))


class _Stream:
    def __init__(self, batch, n_lat, n_ctx, d, nb):
        self.batch, self.n_lat, self.n_ctx, self.d = batch, n_lat, n_ctx, d
        self.nb = nb if batch % nb == 0 else 1
        self.grid = (batch // self.nb, n_lat + n_ctx)
        self.rows = (n_lat + n_ctx) * TILE

    def tile(self, width, col=0):
        return pl.BlockSpec((self.nb, TILE, width), lambda b, t: (b, t, col))

    def sources(self, lat, ctx):
        n_lat = self.n_lat
        off = n_lat if ctx is lat else 0
        return [pl.BlockSpec((self.nb, TILE, self.d), lambda b, t: (b, jnp.minimum(t, n_lat - 1), 0)),
                pl.BlockSpec((self.nb, TILE, self.d), lambda b, t: (b, off + jnp.maximum(t - n_lat, 0), 0))]

    def is_ctx(self):
        return pl.program_id(1) >= self.n_lat

    def read(self, lat_ref, ctx_ref):
        return [jnp.where(self.is_ctx(), ctx_ref[r], lat_ref[r]) for r in range(self.nb)]

    def mod(self):
        return [pl.BlockSpec((self.nb, 1, 6 * self.d), lambda b, t: (b, 0, 0)),
                pl.BlockSpec((None, 1, 6 * self.d), lambda b, t: (self.batch, 0, 0))]

    def read_mod(self, lat_ref, ctx_ref):
        return [jnp.where(self.is_ctx(), ctx_ref[...], lat_ref[r]) for r in range(self.nb)]

    def row(self, width):
        return pl.BlockSpec((1, width), lambda b, t: (0, 0))


def _attn_proj_kernel(x_ref, c_ref, modl_ref, modc_ref, gain_ref, w_ref, qg_ref, kg_ref, cos_ref, sin_ref, gm_ref,
                      rot_ref, q_ref, k_ref, v_ref, *, st):
    d, nb = st.d, st.nb
    mods = st.read_mod(modl_ref, modc_ref)
    h = jnp.concatenate([_norm_mod(x, gain_ref[...], m[:, d:2 * d], m[:, 0:d]).astype(BF16)
                         for x, m in zip(st.read(x_ref, c_ref), mods)], axis=0)
    acc = _dot(h, w_ref[...])
    cos = jnp.concatenate([cos_ref[...]] * nb, axis=0)
    sin = jnp.concatenate([sin_ref[...]] * nb, axis=0)
    gm, rot = gm_ref[...], rot_ref[...]
    blk = 4 * HEAD_DIM
    n_qblk = N_HEADS * HEAD_DIM // blk

    def put(ref, cs, val):
        for r in range(nb):
            ref[r, :, cs] = val[r * TILE:(r + 1) * TILE]

    for c in range(n_qblk + 1):
        xc = acc[:, c * blk:(c + 1) * blk]
        ms = _dot((xc * xc).astype(BF16), gm)
        gain = qg_ref[...] if c < n_qblk else kg_ref[...]
        xn = xc * lax.rsqrt(ms + EPS) * gain
        y = xn * cos + _dot(xn.astype(BF16), rot) * sin
        if c < n_qblk:
            put(q_ref, slice(c * blk, (c + 1) * blk), (y * (HEAD_DIM ** -0.5 * LOG2E)).astype(BF16))
        else:
            put(k_ref, slice(None), y.astype(BF16))
    put(v_ref, slice(None), acc[:, (n_qblk + 1) * blk:].astype(BF16))


def _attn_proj(st, lat, ctx, mod, gain, w, qg, kg, cos, sin, gm, rot):
    b, d = st.batch, st.d
    t = st.rows
    nq, nk = N_HEADS * HEAD_DIM, N_KV_HEADS * HEAD_DIM
    return pl.pallas_call(
        functools.partial(_attn_proj_kernel, st=st),
        grid=st.grid,
        in_specs=[*st.sources(lat, ctx), *st.mod(), st.row(d), _resident(w.shape), st.row(nk), st.row(nk),
                  pl.BlockSpec((TILE, nk), lambda b, t: (t, 0)), pl.BlockSpec((TILE, nk), lambda b, t: (t, 0)),
                  _resident(gm.shape), _resident(rot.shape)],
        out_specs=[st.tile(nq), st.tile(nk), st.tile(nk)],
        out_shape=[jax.ShapeDtypeStruct((b, t, nq), BF16), jax.ShapeDtypeStruct((b, t, nk), BF16),
                   jax.ShapeDtypeStruct((b, t, nk), BF16)],
        compiler_params=_params(("parallel", "parallel")),
        name="attn_proj",
    )(lat, ctx, mod, mod, gain, w, qg, kg, cos, sin, gm, rot)


def _attn_kernel(sink_ref, x_ref, c_ref, modl_ref, modc_ref, q_ref, k_ref, v_ref, w_ref, y_ref, o_ref,
                 *, seq, ctx, d):
    n = pl.program_id(1)
    n_lat = seq // QBLK
    wlen = QBLK + 2 * WINDOW
    is_ctx = n >= n_lat
    ws = pl.multiple_of(jnp.clip(n * QBLK - WINDOW, 0, seq - wlen), QBLK)
    nkeys = wlen + ctx
    qpos = n * QBLK + lax.broadcasted_iota(jnp.int32, (QBLK, wlen), 0)
    kpos = ws + lax.broadcasted_iota(jnp.int32, (QBLK, wlen), 1)
    in_window = (jnp.abs(qpos - kpos) <= WINDOW) & jnp.logical_not(is_ctx)
    bias = jnp.where(in_window, 0.0, NEG_BIG).astype(F32)
    left = lax.broadcasted_iota(jnp.int32, (1, LANES), 1) < HEAD_DIM
    kv_of_lane = lax.broadcasted_iota(jnp.int32, (1, N_KV_HEADS * HEAD_DIM), 1) // HEAD_DIM
    zero = jnp.zeros((), BF16)
    blk = N_KV_HEADS * HEAD_DIM
    nb = q_ref.shape[0]
    for r, g in [(r, g) for r in range(nb) for g in range(GROUP)]:
        if g == 0:
            kall = jnp.concatenate([k_ref[r, pl.ds(ws, wlen), :], k_ref[r, seq:seq + ctx, :]], axis=0)
            vall = jnp.concatenate([v_ref[r, pl.ds(ws, wlen), :], v_ref[r, seq:seq + ctx, :]], axis=0)
            kbd = [jnp.concatenate([jnp.where(left, kall[:, p * LANES:(p + 1) * LANES], zero),
                                    jnp.where(left, zero, kall[:, p * LANES:(p + 1) * LANES])], axis=0)
                   for p in range(N_KV_HEADS // 2)]
            vbd = jnp.concatenate([jnp.where(kv_of_lane == kh, vall, zero) for kh in range(N_KV_HEADS)], axis=0)
        probs, invs = [], []
        for p in range(N_KV_HEADS // 2):
            q2 = q_ref[r, :, g * blk + p * LANES:g * blk + (p + 1) * LANES]
            s = _dot_nt(q2, kbd[p])
            for half in range(2):
                sink = sink_ref[g * N_KV_HEADS + 2 * p + half] * LOG2E
                sw = s[:, half * nkeys:half * nkeys + wlen] + bias
                sc = s[:, half * nkeys + wlen:(half + 1) * nkeys]
                m = jnp.maximum(jnp.maximum(jnp.max(sw, axis=-1, keepdims=True),
                                            jnp.max(sc, axis=-1, keepdims=True)), sink)
                pw, pc = jnp.exp2(sw - m), jnp.exp2(sc - m)
                den = jnp.sum(pw, axis=-1, keepdims=True) + jnp.sum(pc, axis=-1, keepdims=True) + jnp.exp2(sink - m)
                probs += [pw.astype(BF16), pc.astype(BF16)]
                invs.append(1.0 / den)
        o = _dot(jnp.concatenate(probs, axis=1), vbd)
        inv = jnp.where(kv_of_lane == 0, invs[0], jnp.where(kv_of_lane == 1, invs[1],
                                                            jnp.where(kv_of_lane == 2, invs[2], invs[3])))
        o_ref[r * QBLK:(r + 1) * QBLK, g * blk:(g + 1) * blk] = (o * inv).astype(BF16)
    acc = _dot(o_ref[...], w_ref[...])
    for r in range(nb):
        x = jnp.where(is_ctx, c_ref[r], x_ref[r])
        g1 = jnp.where(is_ctx, modc_ref[...], modl_ref[r])[:, 2 * d:3 * d]
        y_ref[r] = x + g1 * acc[r * QBLK:(r + 1) * QBLK]


def _attention(sink, lat, con, mod, q, k, v, w_out, seq, ctx):
    b, t, nq = q.shape
    nk = k.shape[-1]
    d = w_out.shape[1]
    n_lat = seq // QBLK
    off = n_lat if con is lat else 0
    nb = ATTN_ROWS if b % ATTN_ROWS == 0 else 1
    return pl.pallas_call(
        functools.partial(_attn_kernel, seq=seq, ctx=ctx, d=d),
        grid=(b // nb, t // QBLK),
        in_specs=[pl.BlockSpec(memory_space=pltpu.SMEM),
                  pl.BlockSpec((nb, QBLK, d), lambda b, n: (b, jnp.minimum(n, n_lat - 1), 0)),
                  pl.BlockSpec((nb, QBLK, d), lambda b, n: (b, off + jnp.maximum(n - n_lat, 0), 0)),
                  pl.BlockSpec((nb, 1, 6 * d), lambda b, n: (b, 0, 0)),
                  pl.BlockSpec((None, 1, 6 * d), lambda b, n: (mod.shape[0] - 1, 0, 0)),
                  pl.BlockSpec((nb, QBLK, nq), lambda b, n: (b, n, 0)),
                  pl.BlockSpec((nb, t, nk), lambda b, n: (b, 0, 0)),
                  pl.BlockSpec((nb, t, nk), lambda b, n: (b, 0, 0)),
                  _resident(w_out.shape)],
        out_specs=pl.BlockSpec((nb, QBLK, d), lambda b, n: (b, n, 0)),
        out_shape=jax.ShapeDtypeStruct((b, t, d), F32),
        scratch_shapes=[pltpu.VMEM((nb * QBLK, nq), BF16)],
        compiler_params=_params(("parallel", "parallel")),
        name="attention",
    )(sink, lat, con, mod, mod, q, k, v, w_out)


PROJ_COLS = 256
PROJ_AHEAD = 2


def _chunk_cumsum(g, reverse):
    n = g.shape[-1]
    groups = CHUNK // SUBLANES
    g = g.reshape(groups, SUBLANES, n)
    r = lax.broadcasted_iota(jnp.int32, g.shape, 1)
    d = 1
    while d < SUBLANES:
        if reverse:
            g = g + jnp.where(r < SUBLANES - d, pltpu.roll(g, SUBLANES - d, axis=1), 0.0)
        else:
            g = g + jnp.where(r >= d, pltpu.roll(g, d, axis=1), 0.0)
        d *= 2
    g = g.reshape(CHUNK, n)
    edge = 0 if reverse else SUBLANES - 1
    parts = [None] * groups
    carry = None
    for j in (reversed(range(groups)) if reverse else range(groups)):
        blk = g[j * SUBLANES:(j + 1) * SUBLANES]
        if carry is not None:
            blk = blk + carry
        parts[j] = blk
        carry = blk[edge:edge + 1]
    return jnp.concatenate(parts, axis=0)


def _hgrn_proj_kernel(x_ref, modl_ref, modc_ref, gain_ref, w_ref, lbl_ref, q_ref, kf_ref, bf_ref, kb_ref, bb_ref,
                      v_ref, gate_ref, acc_ref, *, st, layer):
    d, nb = st.d, st.nb
    h = jnp.concatenate([_norm_mod(x_ref[r], gain_ref[...], m[:, d:2 * d], m[:, 0:d]).astype(BF16)
                         for r, m in enumerate(st.read_mod(modl_ref, modc_ref))], axis=0)
    lg = lbl_ref[...]
    e = jnp.exp(lg - jnp.max(lg, axis=0, keepdims=True))
    lb = jnp.sum(e[1:layer + 1], axis=0, keepdims=True) / jnp.sum(e, axis=0, keepdims=True)
    w = HG_HEADS * HG_DK

    def put(ref, cs, val):
        for r in range(nb):
            ref[r, :, cs] = val[r * TILE:(r + 1) * TILE]

    def finish(seg, c, acc):
        cs = slice(c * PROJ_COLS, (c + 1) * PROJ_COLS)
        if seg == 0:
            put(q_ref, cs, _silu(acc).astype(BF16))
        elif seg in (1, 2):
            k_ref, b_ref = (kf_ref, bf_ref) if seg == 1 else (kb_ref, bb_ref)
            f = lb[:, cs] + (1.0 - lb[:, cs]) * _sigmoid(acc)
            put(k_ref, cs, (1.0 - f).astype(BF16))
            g = jnp.log2(f)
            for r in range(nb):
                for j in range(TILE // CHUNK):
                    rows = slice(r * TILE + j * CHUNK, r * TILE + (j + 1) * CHUNK)
                    b_ref[r, j * CHUNK:(j + 1) * CHUNK, cs] = _chunk_cumsum(g[rows], reverse=seg == 2)
        elif seg == 3:
            put(v_ref, cs, acc.astype(BF16))
        else:
            put(gate_ref, cs, _silu(acc).astype(BF16))

    base = jnp.minimum(pl.program_id(0), 0)
    items = [(seg, c) for c in range(w // PROJ_COLS) for seg in (1, 2, 0, 4, 3)]
    slots = acc_ref.shape[0]
    for i in range(len(items) + PROJ_AHEAD):
        if i < len(items):
            seg, c = items[i]
            acc_ref[base + i % slots] = _dot(h, w_ref[:, seg * w + c * PROJ_COLS:seg * w + (c + 1) * PROJ_COLS])
        if i >= PROJ_AHEAD:
            finish(*items[i - PROJ_AHEAD], acc_ref[base + (i - PROJ_AHEAD) % slots])


def _hgrn_proj(st, xs, mod, gain, w, lb_logits, layer):
    b, t, d = xs.shape
    wd = HG_HEADS * HG_DK
    assert wd == d
    shapes = [(wd, BF16), (wd, BF16), (wd, F32), (wd, BF16), (wd, F32), (d, BF16), (d, BF16)]
    return pl.pallas_call(
        functools.partial(_hgrn_proj_kernel, st=st, layer=layer),
        grid=st.grid,
        in_specs=[st.tile(d), *st.mod(), st.row(d), _resident(w.shape), _resident(lb_logits.shape)],
        out_specs=[st.tile(n) for n, _ in shapes],
        out_shape=[jax.ShapeDtypeStruct((b, t, n), dt) for n, dt in shapes],
        scratch_shapes=[pltpu.VMEM((PROJ_AHEAD + 1, st.nb * TILE, PROJ_COLS), F32)],
        compiler_params=_params(("parallel", "parallel")),
        name="hgrn_proj",
    )(xs, mod, mod, gain, w, lb_logits)


def _level_table(reverse):
    t = np.arange(CHUNK)[:, None]
    s = np.arange(CHUNK)[None, :]
    x = t ^ s
    lvl = np.floor(np.log2(np.maximum(x, 1))).astype(np.int32)
    later = (t < s) if reverse else (t > s)
    lvl = np.where(later, lvl, -1)
    for i in range(CHUNK):
        j = i if reverse else ~i & (CHUNK - 1)
        lvl[i, i] = N_LEVELS if j == 0 else (j & -j).bit_length() - 1
    return np.concatenate([lvl, lvl], axis=1).astype(np.float32)


SLAB = 16


def _neg_abs16(x):
    sign = jnp.int16(-2 ** 15)
    return lax.bitcast_convert_type(lax.bitcast_convert_type(x, jnp.int16) | sign, BF16)


def _mid_rows(b2, half, reverse):
    n = b2.shape[-1]
    blk = 2 * half
    m = half if reverse else half - 1
    if half == 1:
        b3 = b2.reshape(CHUNK // SUBLANES, SUBLANES, n)
        odd = lax.broadcasted_iota(jnp.int32, b3.shape, 1) % 2 == 1
        if reverse:
            return jnp.where(odd, b3, pltpu.roll(b3, SUBLANES - 1, axis=1)).reshape(CHUNK, n)
        return jnp.where(odd, pltpu.roll(b3, 1, axis=1), b3).reshape(CHUNK, n)
    if blk >= SUBLANES:
        b3 = b2.reshape(CHUNK // blk, blk, n)
        return jnp.broadcast_to(b3[:, m:m + 1, :], b3.shape).reshape(CHUNK, n)
    b3 = b2.reshape(CHUNK // SUBLANES, SUBLANES, n)
    r = lax.broadcasted_iota(jnp.int32, b3.shape, 1)
    mid = jnp.broadcast_to(b3[:, m:m + 1, :], b3.shape)
    for j in range(1, SUBLANES // blk):
        mid = jnp.where(r >= j * blk, jnp.broadcast_to(b3[:, j * blk + m:j * blk + m + 1, :], b3.shape), mid)
    return mid.reshape(CHUNK, n)


def _scan_chunk(lvl_ref, q_ref, k_ref, b_ref, v_ref, o_ref, st_ref, reverse):
    @pl.when(pl.program_id(1) == 0)
    def _():
        st_ref[...] = jnp.zeros(st_ref.shape, F32)

    pw = 2 * HG_DK
    lvl = lvl_ref[...]
    lane = lax.broadcasted_iota(jnp.int32, (1, pw), 1)
    same_head = (lax.broadcasted_iota(jnp.int32, (pw, pw), 0) < HG_DK) == (lane < HG_DK)
    zero = jnp.zeros((), BF16)
    edge = 0 if reverse else CHUNK - 1
    slab = slice(0, SLAB) if reverse else slice(CHUNK - SLAB, CHUNK)

    def pair_rows(x):
        z = jnp.zeros((CHUNK, HG_DK), x.dtype)
        return jnp.concatenate([jnp.concatenate([x[:, :HG_DK], z], axis=1),
                                jnp.concatenate([z, x[:, HG_DK:]], axis=1)], axis=0)

    def pair_cols(xt):
        z = jnp.zeros((HG_DK, CHUNK), xt.dtype)
        return jnp.concatenate([jnp.concatenate([xt[:HG_DK], z], axis=1),
                                jnp.concatenate([z, xt[HG_DK:]], axis=1)], axis=0)

    for r, p in [(r, p) for r in range(q_ref.shape[0]) for p in range(HG_HEADS // 2)]:
        cs = slice(p * pw, (p + 1) * pw)
        q2 = q_ref[r, :, cs]
        k2 = k_ref[r, :, cs]
        b2 = b_ref[r, :, cs]
        v2 = v_ref[r, :, cs]
        b_edge = b2[edge:edge + 1, :]
        st = st_ref[r, p]
        o = _dot_nt(q2 * jnp.exp2(b2).astype(BF16), st.astype(BF16))
        qk = q2[slab].astype(F32) * k2[slab].astype(F32)
        diag = jnp.where(lane < CHUNK, jnp.sum(qk[:, :HG_DK], axis=-1, keepdims=True),
                         jnp.sum(qk[:, HG_DK:], axis=-1, keepdims=True))
        last = jnp.where(lvl[slab] == N_LEVELS, diag.astype(BF16), zero)
        rest = jnp.zeros((CHUNK - SLAB, pw), BF16)
        a = jnp.concatenate([last, rest] if reverse else [rest, last], axis=0)
        kt = k2.T
        for l in range(N_LEVELS):
            e = jnp.exp2(_neg_abs16((b2 - _mid_rows(b2, 2 ** l, reverse)).astype(BF16)))
            a_l = _dot(q2 * e, pair_cols(kt * e.T))
            a = jnp.where(lvl == l, a_l.astype(BF16), a)
        o = o + _dot(a, pair_rows(v2))
        o_ref[r, :, cs] = o.astype(o_ref.dtype)
        kdec = k2 * jnp.exp2(b_edge - b2).astype(BF16)
        upd = _dot_tn(v2, kdec)
        st_ref[r, p] = st * jnp.exp2(b_edge) + jnp.where(same_head, upd, 0.0)


def _scan_fwd_kernel(lvl_ref, q_ref, k_ref, b_ref, v_ref, o_ref, st_ref):
    _scan_chunk(lvl_ref, q_ref, k_ref, b_ref, v_ref, o_ref, st_ref, reverse=False)


def _scan_bwd_out_kernel(lvl_ref, q_ref, k_ref, b_ref, v_ref, of_ref, gate_ref, x_ref, modl_ref, modc_ref, og_ref,
                         w_ref, y_ref, st_ref, ob_ref, *, n_ctx, d):
    _scan_chunk(lvl_ref, q_ref, k_ref, b_ref, v_ref, ob_ref, st_ref, reverse=True)
    rows = q_ref.shape[0]
    og = og_ref[...]
    parts = []
    for r in range(rows):
        o = of_ref[r].astype(F32) + ob_ref[r]
        heads = []
        for hd in range(HG_HEADS):
            oh = o[:, hd * HG_DK:(hd + 1) * HG_DK]
            ms = jnp.mean(oh * oh, axis=-1, keepdims=True)
            heads.append(oh * lax.rsqrt(ms + EPS) * og)
        parts.append((jnp.concatenate(heads, axis=1) * gate_ref[r].astype(F32)).astype(BF16))
    acc = _dot(jnp.concatenate(parts, axis=0), w_ref[...])
    is_ctx = pl.program_id(1) < n_ctx
    for r in range(rows):
        g1 = jnp.where(is_ctx, modc_ref[...], modl_ref[r])[:, 2 * d:3 * d]
        y_ref[r] = x_ref[r] + g1 * acc[r * CHUNK:(r + 1) * CHUNK]


SCAN_ROWS_FWD = 8
SCAN_ROWS_BWD = 4


def _scan_specs(b, w, seq, ctx, reverse):
    n_lat, n_ctx = seq // CHUNK, ctx // CHUNK
    n = n_lat + n_ctx

    def chunk(i):
        if reverse:
            return n - 1 - i
        return jnp.where(i < n_ctx, n_lat + i, i - n_ctx)

    rows = SCAN_ROWS_BWD if reverse else SCAN_ROWS_FWD
    rows = rows if b % rows == 0 else 1
    return rows, n, pl.BlockSpec((rows, CHUNK, w), lambda b, i: (b, chunk(i), 0))


def _hgrn_scan_fwd(lvl, q, k, bcum, v, seq, ctx):
    b, t, w = q.shape
    rows, n, spec = _scan_specs(b, w, seq, ctx, reverse=False)
    return pl.pallas_call(
        _scan_fwd_kernel,
        grid=(b // rows, n),
        in_specs=[_resident(lvl.shape), spec, spec, spec, spec],
        out_specs=spec,
        out_shape=jax.ShapeDtypeStruct((b, t, w), BF16),
        scratch_shapes=[pltpu.VMEM((rows, HG_HEADS // 2, 2 * HG_DK, 2 * HG_DK), F32)],
        compiler_params=_params(("parallel", "arbitrary")),
        name="hgrn_scan_fwd",
    )(lvl, q, k, bcum, v)


def _hgrn_scan_bwd_out(lvl, q, k, bcum, v, o_f, gate, xs, mod, o_gain, w_out, seq, ctx):
    b, t, w = q.shape
    d = w_out.shape[1]
    assert w == d
    rows, n, spec = _scan_specs(b, w, seq, ctx, reverse=True)
    return pl.pallas_call(
        functools.partial(_scan_bwd_out_kernel, n_ctx=ctx // CHUNK, d=d),
        grid=(b // rows, n),
        in_specs=[_resident(lvl.shape), spec, spec, spec, spec, spec, spec, spec,
                  pl.BlockSpec((rows, 1, 6 * d), lambda b, i: (b, 0, 0)),
                  pl.BlockSpec((None, 1, 6 * d), lambda b, i: (mod.shape[0] - 1, 0, 0)),
                  pl.BlockSpec((1, HG_DK), lambda b, i: (0, 0)), _resident(w_out.shape)],
        out_specs=spec,
        out_shape=jax.ShapeDtypeStruct((b, t, d), F32),
        scratch_shapes=[pltpu.VMEM((rows, HG_HEADS // 2, 2 * HG_DK, 2 * HG_DK), F32),
                        pltpu.VMEM((rows, CHUNK, w), F32)],
        compiler_params=_params(("parallel", "arbitrary")),
        name="hgrn_scan_bwd_out",
    )(lvl, q, k, bcum, v, o_f, gate, xs, mod, mod, o_gain, w_out)


FFN_COLS = 256


def _ffn_kernel(x_ref, prev_ref, next_ref, modl_ref, modc_ref, gain_ref, wu_ref, cw_ref, cb_ref, wd_ref, y_ref, u_ref,
                *, st):
    d, nb, n_lat, n_ctx = st.d, st.nb, st.n_lat, st.n_ctx
    t = pl.program_id(1)
    has_prev = jnp.logical_and(t != 0, t != n_lat).astype(F32)
    has_next = jnp.logical_and(t != n_lat - 1, t != n_lat + n_ctx - 1).astype(F32)
    keep = jnp.where(lax.broadcasted_iota(jnp.int32, (2 * SUBLANES, 1), 0) < SUBLANES, has_prev, has_next)
    gain = gain_ref[...]
    mods = st.read_mod(modl_ref, modc_ref)
    xs = [x_ref[r] for r in range(nb)]
    h = jnp.concatenate([_norm_mod(x, gain, m[:, 4 * d:5 * d], m[:, 3 * d:4 * d]).astype(BF16)
                         for x, m in zip(xs, mods)], axis=0)
    halos = [(_norm_mod(jnp.concatenate([prev_ref[r], next_ref[r]], axis=0), gain, m[:, 4 * d:5 * d],
                        m[:, 3 * d:4 * d]) * keep).astype(BF16) for r, m in enumerate(mods)]
    h_ext = jnp.concatenate([h] + halos, axis=0)
    f = wd_ref.shape[0]
    row = lax.broadcasted_iota(jnp.int32, (TILE, 1), 0)
    for c in range(f // FFN_COLS):
        cs = slice(c * FFN_COLS, (c + 1) * FFN_COLS)
        ge = _dot(h_ext, wu_ref[:, cs])
        val = _dot(h, wu_ref[:, f + c * FFN_COLS:f + (c + 1) * FFN_COLS])
        for r in range(nb):
            g = ge[r * TILE:(r + 1) * TILE]
            hb = nb * TILE + 2 * SUBLANES * r
            before = ge[hb + SUBLANES - 1:hb + SUBLANES]
            after = ge[hb + SUBLANES:hb + SUBLANES + 1]
            up = jnp.where(row == 0, before, pltpu.roll(g, 1, axis=0))
            dn = jnp.where(row == TILE - 1, after, pltpu.roll(g, TILE - 1, axis=0))
            conv = up * cw_ref[0:1, cs] + g * cw_ref[1:2, cs] + dn * cw_ref[2:3, cs] + cb_ref[:, cs]
            u_ref[r * TILE:(r + 1) * TILE, cs] = (_silu(conv) * val[r * TILE:(r + 1) * TILE]).astype(BF16)
    acc = _dot(u_ref[...], wd_ref[...])
    for r in range(nb):
        y_ref[r] = xs[r] + mods[r][:, 5 * d:6 * d] * acc[r * TILE:(r + 1) * TILE]


def _ffn(st, xs, mod, gain, w_up, conv_w, conv_b, w_down):
    d = st.d
    f = w_down.shape[0]
    per_tile = TILE // SUBLANES
    last = st.rows // SUBLANES - 1
    return pl.pallas_call(
        functools.partial(_ffn_kernel, st=st),
        grid=st.grid,
        in_specs=[st.tile(d),
                  pl.BlockSpec((st.nb, SUBLANES, d), lambda b, t: (b, jnp.maximum(t * per_tile - 1, 0), 0)),
                  pl.BlockSpec((st.nb, SUBLANES, d), lambda b, t: (b, jnp.minimum((t + 1) * per_tile, last), 0)),
                  *st.mod(), st.row(d), _resident(w_up.shape), _resident(conv_w.shape), _resident(conv_b.shape),
                  _resident(w_down.shape)],
        out_specs=st.tile(d),
        out_shape=jax.ShapeDtypeStruct((st.batch, st.rows, d), F32),
        scratch_shapes=[pltpu.VMEM((st.nb * TILE, f), BF16)],
        compiler_params=_params(("parallel", "parallel")),
        name="ffn",
    )(xs, xs, xs, mod, mod, gain, w_up, conv_w, conv_b, w_down)


def _rope_tables(seq, ctx):
    rows = seq // GRID_W
    row = jnp.repeat(jnp.arange(rows, dtype=F32), GRID_W)
    col = jnp.tile(jnp.arange(GRID_W, dtype=F32), rows)
    n_pairs = HEAD_DIM // 4
    inv = ROPE_THETA ** (-jnp.arange(n_pairs, dtype=F32) / n_pairs)
    ang_r, ang_c = row[:, None] * inv, col[:, None] * inv
    cos = jnp.concatenate([jnp.cos(ang_r)] * 2 + [jnp.cos(ang_c)] * 2, axis=1)
    sin = jnp.concatenate([-jnp.sin(ang_r), jnp.sin(ang_r), -jnp.sin(ang_c), jnp.sin(ang_c)], axis=1)
    cos = jnp.concatenate([cos, jnp.ones((ctx, HEAD_DIM), F32)], axis=0)
    sin = jnp.concatenate([sin, jnp.zeros((ctx, HEAD_DIM), F32)], axis=0)
    return jnp.tile(cos, (1, 4)), jnp.tile(sin, (1, 4))


def _head_block_constants():
    blk = 4 * HEAD_DIM
    i = np.arange(blk)
    gm = (i[:, None] // HEAD_DIM == i[None, :] // HEAD_DIM).astype(np.float32) / HEAD_DIM
    quarter = HEAD_DIM // 4
    partner = np.where((i % (2 * quarter)) < quarter, i + quarter, i - quarter)
    rot = (i[:, None] == partner[None, :]).astype(np.float32)
    return jnp.asarray(gm, BF16), jnp.asarray(rot, BF16)


def _query_head_order():
    return np.array([kh * GROUP + g for g in range(GROUP) for kh in range(N_KV_HEADS)])


def kernel(x, c, ctx, c_ctx, ada_w, ada_b, norm1_g, norm2_g, attn_w_in, attn_w_out, attn_q_gain, attn_k_gain,
           attn_sink, hgrn_w_in, hgrn_w_out, hgrn_o_gain, hgrn_lb_logits, ffn_w_up, ffn_conv_w, ffn_conv_b,
           ffn_w_down):
    batch, seq, d = x.shape
    n_ctx_tok = ctx.shape[1]
    depth = ada_w.shape[0]
    assert seq % TILE == 0 and n_ctx_tok % TILE == 0 and seq >= QBLK + 2 * WINDOW
    n_lat, n_con = seq // TILE, n_ctx_tok // TILE
    st2 = _Stream(batch, n_lat, n_con, d, STREAM_ROWS)
    st4 = _Stream(batch, n_lat, n_con, d, WIDE_ROWS)
    st4_lat = _Stream(batch, n_lat, 0, d, WIDE_ROWS)

    rows = -(-(batch + 1) // SUBLANES) * SUBLANES
    cc = jnp.zeros((rows, d), F32).at[:batch].set(c).at[batch].set(c_ctx)
    mod_all = _modulation(cc, ada_w, ada_b)[:, :batch + 1].reshape(depth, batch + 1, 1, 6 * d)

    cos, sin = _rope_tables(seq, n_ctx_tok)
    gm, rot = _head_block_constants()
    order = _query_head_order()
    nq = N_HEADS * HEAD_DIM
    lvl_f, lvl_b = jnp.asarray(_level_table(False), BF16), jnp.asarray(_level_table(True), BF16)

    lat, con = x, ctx
    for layer in range(depth):
        j = layer // 2
        mod = mod_all[layer]
        g1 = norm1_g[layer].reshape(1, d)
        st_out = st4_lat if layer == depth - 1 else st4
        if layer % 2 == 0:
            w_in = attn_w_in[j]
            wq = w_in[:, :nq].reshape(d, N_HEADS, HEAD_DIM)[:, order].reshape(d, nq)
            w_in = jnp.concatenate([wq, w_in[:, nq:]], axis=1).astype(BF16)
            w_out = attn_w_out[j].reshape(N_HEADS, HEAD_DIM, d)[order].reshape(nq, d).astype(BF16)
            qg = jnp.tile(attn_q_gain[j], 4).reshape(1, 4 * HEAD_DIM)
            kg = jnp.tile(attn_k_gain[j], 4).reshape(1, 4 * HEAD_DIM)
            q, k, v = _attn_proj(st4, lat, con, mod, g1, w_in, qg, kg, cos, sin, gm, rot)
            xs = _attention(attn_sink[j][order], lat, con, mod, q, k, v, w_out, seq, n_ctx_tok)
        else:
            q, kf, bf, kb, bb, v, gate = _hgrn_proj(st2, lat, mod, g1, hgrn_w_in[j].astype(BF16), hgrn_lb_logits,
                                                    layer)
            o_f = _hgrn_scan_fwd(lvl_f, q, kf, bf, v, seq, n_ctx_tok)
            xs = _hgrn_scan_bwd_out(lvl_b, q, kb, bb, v, o_f, gate, lat, mod, hgrn_o_gain[j].reshape(1, HG_DK),
                                    hgrn_w_out[j].astype(BF16), seq, n_ctx_tok)
        xs = _ffn(st_out, xs, mod, norm2_g[layer].reshape(1, d), ffn_w_up[layer].astype(BF16), ffn_conv_w[layer],
                  ffn_conv_b[layer].reshape(1, -1), ffn_w_down[layer].astype(BF16))
        lat = con = xs
    return xs
```

```python
import functools
import math

import jax
import jax.numpy as jnp
import numpy as np
from jax import lax
from jax.experimental import pallas as pl
from jax.experimental.pallas import tpu as pltpu

F32 = jnp.float32
BF16 = jnp.bfloat16

EPS = 1e-6
HEAD_DIM = 64
N_HEADS = 16
N_KV_HEADS = 4
GROUP = N_HEADS // N_KV_HEADS
WINDOW = 128
ROPE_THETA = 10000.0
GRID_W = 64
HG_HEADS = 8
HG_DK = 128
CHUNK = 128
N_LEVELS = 7
TILE = 256
STREAM_ROWS = 2
WIDE_ROWS = 4
QBLK = 128
ATTN_ROWS = 2
SM_ROWS = 32
LANES = 128
SUBLANES = 8
NEG_BIG = -1e30
LOG2E = math.log2(math.e)
VMEM_LIMIT = 56 * 1024 * 1024


def _silu(x):
    return x / (1.0 + jnp.exp(-x))


def _sigmoid(x):
    return 1.0 / (1.0 + jnp.exp(-x))


def _dot(a, b):
    return jnp.dot(a, b, preferred_element_type=F32)


def _dot_nt(a, b):
    return lax.dot_general(a, b, (((1,), (1,)), ((), ())), preferred_element_type=F32)


def _dot_tn(a, b):
    return lax.dot_general(a, b, (((0,), (0,)), ((), ())), preferred_element_type=F32)


def _norm_mod(x, gain, scale, shift):
    ms = jnp.mean(x * x, axis=-1, keepdims=True)
    return (x * lax.rsqrt(ms + EPS)) * gain * (1.0 + scale) + shift


def _params(sem):
    return pltpu.CompilerParams(dimension_semantics=sem, vmem_limit_bytes=VMEM_LIMIT)


def _resident(shape):
    nd = len(shape)
    return pl.BlockSpec(shape, lambda *_: (0,) * nd, pipeline_mode=pl.Buffered(1))


MOD_COLS = 1536


def _mod_kernel(cc_ref, w_ref, b_ref, o_ref):
    a = _silu(cc_ref[...]).astype(BF16)
    o_ref[...] = _dot(a, w_ref[...].astype(BF16)) + b_ref[...]


def _modulation(cc, ada_w, ada_b):
    depth, d, n = ada_w.shape
    r = cc.shape[0]
    tn = MOD_COLS
    return pl.pallas_call(
        _mod_kernel,
        grid=(depth, n // tn),
        in_specs=[
            pl.BlockSpec((r, d), lambda l, j: (0, 0)),
            pl.BlockSpec((None, d, tn), lambda l, j: (l, 0, j)),
            pl.BlockSpec((None, 1, tn), lambda l, j: (l, 0, j)),
        ],
        out_specs=pl.BlockSpec((None, r, tn), lambda l, j: (l, 0, j)),
        out_shape=jax.ShapeDtypeStruct((depth, r, n), F32),
        compiler_params=_params(("parallel", "parallel")),
        name="modulation",
    )(cc, ada_w, ada_b.reshape(depth, 1, n))


class _Stream:
    def __init__(self, batch, n_lat, n_ctx, d, nb):
        self.batch, self.n_lat, self.n_ctx, self.d = batch, n_lat, n_ctx, d
        self.nb = nb if batch % nb == 0 else 1
        self.grid = (batch // self.nb, n_lat + n_ctx)
        self.rows = (n_lat + n_ctx) * TILE

    def tile(self, width, col=0):
        return pl.BlockSpec((self.nb, TILE, width), lambda b, t: (b, t, col))

    def sources(self, lat, ctx):
        n_lat = self.n_lat
        off = n_lat if ctx is lat else 0
        return [pl.BlockSpec((self.nb, TILE, self.d), lambda b, t: (b, jnp.minimum(t, n_lat - 1), 0)),
                pl.BlockSpec((self.nb, TILE, self.d), lambda b, t: (b, off + jnp.maximum(t - n_lat, 0), 0))]

    def is_ctx(self):
        return pl.program_id(1) >= self.n_lat

    def read(self, lat_ref, ctx_ref):
        return [jnp.where(self.is_ctx(), ctx_ref[r], lat_ref[r]) for r in range(self.nb)]

    def mod(self):
        return [pl.BlockSpec((self.nb, 1, 6 * self.d), lambda b, t: (b, 0, 0)),
                pl.BlockSpec((None, 1, 6 * self.d), lambda b, t: (self.batch, 0, 0))]

    def read_mod(self, lat_ref, ctx_ref):
        return [jnp.where(self.is_ctx(), ctx_ref[...], lat_ref[r]) for r in range(self.nb)]

    def row(self, width):
        return pl.BlockSpec((1, width), lambda b, t: (0, 0))


def _attn_proj_kernel(x_ref, c_ref, modl_ref, modc_ref, gain_ref, w_ref, qg_ref, kg_ref, cos_ref, sin_ref, gm_ref,
                      rot_ref, q_ref, k_ref, v_ref, *, st):
    d, nb = st.d, st.nb
    mods = st.read_mod(modl_ref, modc_ref)
    h = jnp.concatenate([_norm_mod(x, gain_ref[...], m[:, d:2 * d], m[:, 0:d]).astype(BF16)
                         for x, m in zip(st.read(x_ref, c_ref), mods)], axis=0)
    acc = _dot(h, w_ref[...])
    cos = jnp.concatenate([cos_ref[...]] * nb, axis=0)
    sin = jnp.concatenate([sin_ref[...]] * nb, axis=0)
    gm, rot = gm_ref[...], rot_ref[...]
    blk = 4 * HEAD_DIM
    n_qblk = N_HEADS * HEAD_DIM // blk

    def put(ref, cs, val):
        for r in range(nb):
            ref[r, :, cs] = val[r * TILE:(r + 1) * TILE]

    for c in range(n_qblk + 1):
        xc = acc[:, c * blk:(c + 1) * blk]
        ms = _dot((xc * xc).astype(BF16), gm)
        gain = qg_ref[...] if c < n_qblk else kg_ref[...]
        xn = xc * lax.rsqrt(ms + EPS) * gain
        y = xn * cos + _dot(xn.astype(BF16), rot) * sin
        if c < n_qblk:
            put(q_ref, slice(c * blk, (c + 1) * blk), (y * (HEAD_DIM ** -0.5 * LOG2E)).astype(BF16))
        else:
            put(k_ref, slice(None), y.astype(BF16))
    put(v_ref, slice(None), acc[:, (n_qblk + 1) * blk:].astype(BF16))


def _attn_proj(st, lat, ctx, mod, gain, w, qg, kg, cos, sin, gm, rot):
    b, d = st.batch, st.d
    t = st.rows
    nq, nk = N_HEADS * HEAD_DIM, N_KV_HEADS * HEAD_DIM
    return pl.pallas_call(
        functools.partial(_attn_proj_kernel, st=st),
        grid=st.grid,
        in_specs=[*st.sources(lat, ctx), *st.mod(), st.row(d), _resident(w.shape), st.row(nk), st.row(nk),
                  pl.BlockSpec((TILE, nk), lambda b, t: (t, 0)), pl.BlockSpec((TILE, nk), lambda b, t: (t, 0)),
                  _resident(gm.shape), _resident(rot.shape)],
        out_specs=[st.tile(nq), st.tile(nk), st.tile(nk)],
        out_shape=[jax.ShapeDtypeStruct((b, t, nq), BF16), jax.ShapeDtypeStruct((b, t, nk), BF16),
                   jax.ShapeDtypeStruct((b, t, nk), BF16)],
        compiler_params=_params(("parallel", "parallel")),
        name="attn_proj",
    )(lat, ctx, mod, mod, gain, w, qg, kg, cos, sin, gm, rot)


def _attn_kernel(sink_ref, x_ref, c_ref, modl_ref, modc_ref, q_ref, k_ref, v_ref, w_ref, y_ref, o_ref, s_ref, p_ref,
                 *, seq, ctx, d):
    n = pl.program_id(1)
    n_lat = seq // QBLK
    wlen = QBLK + 2 * WINDOW
    is_ctx = n >= n_lat
    ws = pl.multiple_of(jnp.clip(n * QBLK - WINDOW, 0, seq - wlen), QBLK)
    nkeys = wlen + ctx
    qpos = n * QBLK + lax.broadcasted_iota(jnp.int32, (QBLK, wlen), 0)
    kpos = ws + lax.broadcasted_iota(jnp.int32, (QBLK, wlen), 1)
    in_window = (jnp.abs(qpos - kpos) <= WINDOW) & jnp.logical_not(is_ctx)
    bias = jnp.where(in_window, 0.0, NEG_BIG).astype(F32)
    left = lax.broadcasted_iota(jnp.int32, (1, LANES), 1) < HEAD_DIM
    kv_of_lane = lax.broadcasted_iota(jnp.int32, (1, N_KV_HEADS * HEAD_DIM), 1) // HEAD_DIM
    zero = jnp.zeros((), BF16)
    blk = N_KV_HEADS * HEAD_DIM
    nb = q_ref.shape[0]
    base = jnp.minimum(pl.program_id(0), 0)
    for r, g in [(r, g) for r in range(nb) for g in range(GROUP)]:
        if g == 0:
            kall = jnp.concatenate([k_ref[r, pl.ds(ws, wlen), :], k_ref[r, seq:seq + ctx, :]], axis=0)
            vall = jnp.concatenate([v_ref[r, pl.ds(ws, wlen), :], v_ref[r, seq:seq + ctx, :]], axis=0)
            kbd = [jnp.concatenate([jnp.where(left, kall[:, p * LANES:(p + 1) * LANES], zero),
                                    jnp.where(left, zero, kall[:, p * LANES:(p + 1) * LANES])], axis=0)
                   for p in range(N_KV_HEADS // 2)]
            vbd = jnp.concatenate([jnp.where(kv_of_lane == kh, vall, zero) for kh in range(N_KV_HEADS)], axis=0)
        idx = r * GROUP + g
        pslot = base + idx % 2
        invs = [[] for _ in range(QBLK // SM_ROWS)]
        for p in range(N_KV_HEADS // 2):
            q2 = q_ref[r, :, g * blk + p * LANES:g * blk + (p + 1) * LANES]
            sslot = base + p
            s_ref[sslot] = _dot_nt(q2, kbd[p])
            for half, i in [(half, i) for half in range(2) for i in range(QBLK // SM_ROWS)]:
                rows = slice(i * SM_ROWS, (i + 1) * SM_ROWS)
                c0 = half * nkeys
                sink = sink_ref[g * N_KV_HEADS + 2 * p + half] * LOG2E
                sw = s_ref[sslot, rows, c0:c0 + wlen] + bias[rows]
                sc = s_ref[sslot, rows, c0 + wlen:c0 + nkeys]
                m = jnp.maximum(jnp.maximum(jnp.max(sw, axis=-1, keepdims=True),
                                            jnp.max(sc, axis=-1, keepdims=True)), sink)
                pw, pc = jnp.exp2(sw - m), jnp.exp2(sc - m)
                den = jnp.sum(pw, axis=-1, keepdims=True) + jnp.sum(pc, axis=-1, keepdims=True) + jnp.exp2(sink - m)
                k0 = (2 * p + half) * nkeys
                p_ref[pslot, rows, k0:k0 + wlen] = pw.astype(BF16)
                p_ref[pslot, rows, k0 + wlen:k0 + nkeys] = pc.astype(BF16)
                invs[i].append(1.0 / den)
        o = _dot(p_ref[pslot], vbd)
        inv = jnp.concatenate([jnp.where(kv_of_lane == 0, v[0], jnp.where(kv_of_lane == 1, v[1],
                               jnp.where(kv_of_lane == 2, v[2], v[3]))) for v in invs], axis=0)
        o_ref[r * QBLK:(r + 1) * QBLK, g * blk:(g + 1) * blk] = (o * inv).astype(BF16)
    acc = _dot(o_ref[...], w_ref[...])
    for r in range(nb):
        x = jnp.where(is_ctx, c_ref[r], x_ref[r])
        g1 = jnp.where(is_ctx, modc_ref[...], modl_ref[r])[:, 2 * d:3 * d]
        y_ref[r] = x + g1 * acc[r * QBLK:(r + 1) * QBLK]


def _attention(sink, lat, con, mod, q, k, v, w_out, seq, ctx):
    b, t, nq = q.shape
    nk = k.shape[-1]
    d = w_out.shape[1]
    n_lat = seq // QBLK
    off = n_lat if con is lat else 0
    nb = ATTN_ROWS if b % ATTN_ROWS == 0 else 1
    return pl.pallas_call(
        functools.partial(_attn_kernel, seq=seq, ctx=ctx, d=d),
        grid=(b // nb, t // QBLK),
        in_specs=[pl.BlockSpec(memory_space=pltpu.SMEM),
                  pl.BlockSpec((nb, QBLK, d), lambda b, n: (b, jnp.minimum(n, n_lat - 1), 0)),
                  pl.BlockSpec((nb, QBLK, d), lambda b, n: (b, off + jnp.maximum(n - n_lat, 0), 0)),
                  pl.BlockSpec((nb, 1, 6 * d), lambda b, n: (b, 0, 0)),
                  pl.BlockSpec((None, 1, 6 * d), lambda b, n: (mod.shape[0] - 1, 0, 0)),
                  pl.BlockSpec((nb, QBLK, nq), lambda b, n: (b, n, 0)),
                  pl.BlockSpec((nb, t, nk), lambda b, n: (b, 0, 0)),
                  pl.BlockSpec((nb, t, nk), lambda b, n: (b, 0, 0)),
                  _resident(w_out.shape)],
        out_specs=pl.BlockSpec((nb, QBLK, d), lambda b, n: (b, n, 0)),
        out_shape=jax.ShapeDtypeStruct((b, t, d), F32),
        scratch_shapes=[pltpu.VMEM((nb * QBLK, nq), BF16),
                        pltpu.VMEM((N_KV_HEADS // 2, QBLK, 2 * (QBLK + 2 * WINDOW + ctx)), F32),
                        pltpu.VMEM((2, QBLK, N_KV_HEADS * (QBLK + 2 * WINDOW + ctx)), BF16)],
        compiler_params=_params(("parallel", "parallel")),
        name="attention",
    )(sink, lat, con, mod, mod, q, k, v, w_out)


PROJ_COLS = 256
PROJ_AHEAD = 2


def _chunk_cumsum(g, reverse):
    n = g.shape[-1]
    groups = CHUNK // SUBLANES
    g = g.reshape(groups, SUBLANES, n)
    r = lax.broadcasted_iota(jnp.int32, g.shape, 1)
    d = 1
    while d < SUBLANES:
        if reverse:
            g = g + jnp.where(r < SUBLANES - d, pltpu.roll(g, SUBLANES - d, axis=1), 0.0)
        else:
            g = g + jnp.where(r >= d, pltpu.roll(g, d, axis=1), 0.0)
        d *= 2
    g = g.reshape(CHUNK, n)
    edge = 0 if reverse else SUBLANES - 1
    parts = [None] * groups
    carry = None
    for j in (reversed(range(groups)) if reverse else range(groups)):
        blk = g[j * SUBLANES:(j + 1) * SUBLANES]
        if carry is not None:
            blk = blk + carry
        parts[j] = blk
        carry = blk[edge:edge + 1]
    return jnp.concatenate(parts, axis=0)


def _hgrn_proj_kernel(x_ref, modl_ref, modc_ref, gain_ref, w_ref, lbl_ref, q_ref, kf_ref, bf_ref, kb_ref, bb_ref,
                      v_ref, gate_ref, acc_ref, *, st, layer):
    d, nb = st.d, st.nb
    h = jnp.concatenate([_norm_mod(x_ref[r], gain_ref[...], m[:, d:2 * d], m[:, 0:d]).astype(BF16)
                         for r, m in enumerate(st.read_mod(modl_ref, modc_ref))], axis=0)
    lg = lbl_ref[...]
    e = jnp.exp(lg - jnp.max(lg, axis=0, keepdims=True))
    lb = jnp.sum(e[1:layer + 1], axis=0, keepdims=True) / jnp.sum(e, axis=0, keepdims=True)
    w = HG_HEADS * HG_DK

    def put(ref, cs, val):
        for r in range(nb):
            ref[r, :, cs] = val[r * TILE:(r + 1) * TILE]

    def finish(seg, c, acc):
        cs = slice(c * PROJ_COLS, (c + 1) * PROJ_COLS)
        if seg == 0:
            put(q_ref, cs, _silu(acc).astype(BF16))
        elif seg in (1, 2):
            k_ref, b_ref = (kf_ref, bf_ref) if seg == 1 else (kb_ref, bb_ref)
            f = lb[:, cs] + (1.0 - lb[:, cs]) * _sigmoid(acc)
            put(k_ref, cs, (1.0 - f).astype(BF16))
            g = jnp.log2(f)
            for r in range(nb):
                for j in range(TILE // CHUNK):
                    rows = slice(r * TILE + j * CHUNK, r * TILE + (j + 1) * CHUNK)
                    b_ref[r, j * CHUNK:(j + 1) * CHUNK, cs] = _chunk_cumsum(g[rows], reverse=seg == 2)
        elif seg == 3:
            put(v_ref, cs, acc.astype(BF16))
        else:
            put(gate_ref, cs, _silu(acc).astype(BF16))

    base = jnp.minimum(pl.program_id(0), 0)
    items = [(seg, c) for c in range(w // PROJ_COLS) for seg in (1, 2, 0, 4, 3)]
    slots = acc_ref.shape[0]
    for i in range(len(items) + PROJ_AHEAD):
        if i < len(items):
            seg, c = items[i]
            acc_ref[base + i % slots] = _dot(h, w_ref[:, seg * w + c * PROJ_COLS:seg * w + (c + 1) * PROJ_COLS])
        if i >= PROJ_AHEAD:
            finish(*items[i - PROJ_AHEAD], acc_ref[base + (i - PROJ_AHEAD) % slots])


def _hgrn_proj(st, xs, mod, gain, w, lb_logits, layer):
    b, t, d = xs.shape
    wd = HG_HEADS * HG_DK
    assert wd == d
    shapes = [(wd, BF16), (wd, BF16), (wd, F32), (wd, BF16), (wd, F32), (d, BF16), (d, BF16)]
    return pl.pallas_call(
        functools.partial(_hgrn_proj_kernel, st=st, layer=layer),
        grid=st.grid,
        in_specs=[st.tile(d), *st.mod(), st.row(d), _resident(w.shape), _resident(lb_logits.shape)],
        out_specs=[st.tile(n) for n, _ in shapes],
        out_shape=[jax.ShapeDtypeStruct((b, t, n), dt) for n, dt in shapes],
        scratch_shapes=[pltpu.VMEM((PROJ_AHEAD + 1, st.nb * TILE, PROJ_COLS), F32)],
        compiler_params=_params(("parallel", "parallel")),
        name="hgrn_proj",
    )(xs, mod, mod, gain, w, lb_logits)


def _level_table(reverse):
    t = np.arange(CHUNK)[:, None]
    s = np.arange(CHUNK)[None, :]
    x = t ^ s
    lvl = np.floor(np.log2(np.maximum(x, 1))).astype(np.int32)
    later = (t < s) if reverse else (t > s)
    lvl = np.where(later, lvl, -1)
    for i in range(CHUNK):
        j = i if reverse else ~i & (CHUNK - 1)
        lvl[i, i] = N_LEVELS if j == 0 else (j & -j).bit_length() - 1
    return np.concatenate([lvl, lvl], axis=1).astype(np.float32)


SLAB = 16


def _neg_abs16(x):
    sign = jnp.int16(-2 ** 15)
    return lax.bitcast_convert_type(lax.bitcast_convert_type(x, jnp.int16) | sign, BF16)


def _mid_rows(b2, half, reverse):
    n = b2.shape[-1]
    blk = 2 * half
    m = half if reverse else half - 1
    if half == 1:
        b3 = b2.reshape(CHUNK // SUBLANES, SUBLANES, n)
        odd = lax.broadcasted_iota(jnp.int32, b3.shape, 1) % 2 == 1
        if reverse:
            return jnp.where(odd, b3, pltpu.roll(b3, SUBLANES - 1, axis=1)).reshape(CHUNK, n)
        return jnp.where(odd, pltpu.roll(b3, 1, axis=1), b3).reshape(CHUNK, n)
    if blk >= SUBLANES:
        b3 = b2.reshape(CHUNK // blk, blk, n)
        return jnp.broadcast_to(b3[:, m:m + 1, :], b3.shape).reshape(CHUNK, n)
    b3 = b2.reshape(CHUNK // SUBLANES, SUBLANES, n)
    r = lax.broadcasted_iota(jnp.int32, b3.shape, 1)
    mid = jnp.broadcast_to(b3[:, m:m + 1, :], b3.shape)
    for j in range(1, SUBLANES // blk):
        mid = jnp.where(r >= j * blk, jnp.broadcast_to(b3[:, j * blk + m:j * blk + m + 1, :], b3.shape), mid)
    return mid.reshape(CHUNK, n)


def _scan_chunk(lvl_ref, q_ref, k_ref, b_ref, v_ref, o_ref, st_ref, reverse):
    @pl.when(pl.program_id(1) == 0)
    def _():
        st_ref[...] = jnp.zeros(st_ref.shape, F32)

    pw = 2 * HG_DK
    lvl = lvl_ref[...]
    lane = lax.broadcasted_iota(jnp.int32, (1, pw), 1)
    same_head = (lax.broadcasted_iota(jnp.int32, (pw, pw), 0) < HG_DK) == (lane < HG_DK)
    zero = jnp.zeros((), BF16)
    edge = 0 if reverse else CHUNK - 1
    slab = slice(0, SLAB) if reverse else slice(CHUNK - SLAB, CHUNK)

    def pair_rows(x):
        z = jnp.zeros((CHUNK, HG_DK), x.dtype)
        return jnp.concatenate([jnp.concatenate([x[:, :HG_DK], z], axis=1),
                                jnp.concatenate([z, x[:, HG_DK:]], axis=1)], axis=0)

    def pair_cols(xt):
        z = jnp.zeros((HG_DK, CHUNK), xt.dtype)
        return jnp.concatenate([jnp.concatenate([xt[:HG_DK], z], axis=1),
                                jnp.concatenate([z, xt[HG_DK:]], axis=1)], axis=0)

    for r, p in [(r, p) for r in range(q_ref.shape[0]) for p in range(HG_HEADS // 2)]:
        cs = slice(p * pw, (p + 1) * pw)
        q2 = q_ref[r, :, cs]
        k2 = k_ref[r, :, cs]
        b2 = b_ref[r, :, cs]
        v2 = v_ref[r, :, cs]
        b_edge = b2[edge:edge + 1, :]
        st = st_ref[r, p]
        o = _dot_nt(q2 * jnp.exp2(b2).astype(BF16), st.astype(BF16))
        qk = q2[slab].astype(F32) * k2[slab].astype(F32)
        diag = jnp.where(lane < CHUNK, jnp.sum(qk[:, :HG_DK], axis=-1, keepdims=True),
                         jnp.sum(qk[:, HG_DK:], axis=-1, keepdims=True))
        last = jnp.where(lvl[slab] == N_LEVELS, diag.astype(BF16), zero)
        rest = jnp.zeros((CHUNK - SLAB, pw), BF16)
        a = jnp.concatenate([last, rest] if reverse else [rest, last], axis=0)
        kt = k2.T
        for l in range(N_LEVELS):
            e = jnp.exp2(_neg_abs16((b2 - _mid_rows(b2, 2 ** l, reverse)).astype(BF16)))
            a_l = _dot(q2 * e, pair_cols(kt * e.T))
            a = jnp.where(lvl == l, a_l.astype(BF16), a)
        o = o + _dot(a, pair_rows(v2))
        o_ref[r, :, cs] = o.astype(o_ref.dtype)
        kdec = k2 * jnp.exp2(b_edge - b2).astype(BF16)
        upd = _dot_tn(v2, kdec)
        st_ref[r, p] = st * jnp.exp2(b_edge) + jnp.where(same_head, upd, 0.0)


def _scan_fwd_kernel(lvl_ref, q_ref, k_ref, b_ref, v_ref, o_ref, st_ref):
    _scan_chunk(lvl_ref, q_ref, k_ref, b_ref, v_ref, o_ref, st_ref, reverse=False)


def _scan_bwd_out_kernel(lvl_ref, q_ref, k_ref, b_ref, v_ref, of_ref, gate_ref, x_ref, modl_ref, modc_ref, og_ref,
                         w_ref, y_ref, st_ref, ob_ref, *, n_ctx, d):
    _scan_chunk(lvl_ref, q_ref, k_ref, b_ref, v_ref, ob_ref, st_ref, reverse=True)
    rows = q_ref.shape[0]
    og = og_ref[...]
    parts = []
    for r in range(rows):
        o = of_ref[r].astype(F32) + ob_ref[r]
        heads = []
        for hd in range(HG_HEADS):
            oh = o[:, hd * HG_DK:(hd + 1) * HG_DK]
            ms = jnp.mean(oh * oh, axis=-1, keepdims=True)
            heads.append(oh * lax.rsqrt(ms + EPS) * og)
        parts.append((jnp.concatenate(heads, axis=1) * gate_ref[r].astype(F32)).astype(BF16))
    acc = _dot(jnp.concatenate(parts, axis=0), w_ref[...])
    is_ctx = pl.program_id(1) < n_ctx
    for r in range(rows):
        g1 = jnp.where(is_ctx, modc_ref[...], modl_ref[r])[:, 2 * d:3 * d]
        y_ref[r] = x_ref[r] + g1 * acc[r * CHUNK:(r + 1) * CHUNK]


SCAN_ROWS_FWD = 8
SCAN_ROWS_BWD = 4


def _scan_specs(b, w, seq, ctx, reverse):
    n_lat, n_ctx = seq // CHUNK, ctx // CHUNK
    n = n_lat + n_ctx

    def chunk(i):
        if reverse:
            return n - 1 - i
        return jnp.where(i < n_ctx, n_lat + i, i - n_ctx)

    rows = SCAN_ROWS_BWD if reverse else SCAN_ROWS_FWD
    rows = rows if b % rows == 0 else 1
    return rows, n, pl.BlockSpec((rows, CHUNK, w), lambda b, i: (b, chunk(i), 0))


def _hgrn_scan_fwd(lvl, q, k, bcum, v, seq, ctx):
    b, t, w = q.shape
    rows, n, spec = _scan_specs(b, w, seq, ctx, reverse=False)
    return pl.pallas_call(
        _scan_fwd_kernel,
        grid=(b // rows, n),
        in_specs=[_resident(lvl.shape), spec, spec, spec, spec],
        out_specs=spec,
        out_shape=jax.ShapeDtypeStruct((b, t, w), BF16),
        scratch_shapes=[pltpu.VMEM((rows, HG_HEADS // 2, 2 * HG_DK, 2 * HG_DK), F32)],
        compiler_params=_params(("parallel", "arbitrary")),
        name="hgrn_scan_fwd",
    )(lvl, q, k, bcum, v)


def _hgrn_scan_bwd_out(lvl, q, k, bcum, v, o_f, gate, xs, mod, o_gain, w_out, seq, ctx):
    b, t, w = q.shape
    d = w_out.shape[1]
    assert w == d
    rows, n, spec = _scan_specs(b, w, seq, ctx, reverse=True)
    return pl.pallas_call(
        functools.partial(_scan_bwd_out_kernel, n_ctx=ctx // CHUNK, d=d),
        grid=(b // rows, n),
        in_specs=[_resident(lvl.shape), spec, spec, spec, spec, spec, spec, spec,
                  pl.BlockSpec((rows, 1, 6 * d), lambda b, i: (b, 0, 0)),
                  pl.BlockSpec((None, 1, 6 * d), lambda b, i: (mod.shape[0] - 1, 0, 0)),
                  pl.BlockSpec((1, HG_DK), lambda b, i: (0, 0)), _resident(w_out.shape)],
        out_specs=spec,
        out_shape=jax.ShapeDtypeStruct((b, t, d), F32),
        scratch_shapes=[pltpu.VMEM((rows, HG_HEADS // 2, 2 * HG_DK, 2 * HG_DK), F32),
                        pltpu.VMEM((rows, CHUNK, w), F32)],
        compiler_params=_params(("parallel", "arbitrary")),
        name="hgrn_scan_bwd_out",
    )(lvl, q, k, bcum, v, o_f, gate, xs, mod, mod, o_gain, w_out)


FFN_COLS = 256


def _ffn_kernel(x_ref, prev_ref, next_ref, modl_ref, modc_ref, gain_ref, wu_ref, cw_ref, cb_ref, wd_ref, y_ref, u_ref,
                *, st):
    d, nb, n_lat, n_ctx = st.d, st.nb, st.n_lat, st.n_ctx
    t = pl.program_id(1)
    has_prev = jnp.logical_and(t != 0, t != n_lat).astype(F32)
    has_next = jnp.logical_and(t != n_lat - 1, t != n_lat + n_ctx - 1).astype(F32)
    keep = jnp.where(lax.broadcasted_iota(jnp.int32, (2 * SUBLANES, 1), 0) < SUBLANES, has_prev, has_next)
    gain = gain_ref[...]
    mods = st.read_mod(modl_ref, modc_ref)
    xs = [x_ref[r] for r in range(nb)]
    h = jnp.concatenate([_norm_mod(x, gain, m[:, 4 * d:5 * d], m[:, 3 * d:4 * d]).astype(BF16)
                         for x, m in zip(xs, mods)], axis=0)
    halos = [(_norm_mod(jnp.concatenate([prev_ref[r], next_ref[r]], axis=0), gain, m[:, 4 * d:5 * d],
                        m[:, 3 * d:4 * d]) * keep).astype(BF16) for r, m in enumerate(mods)]
    h_ext = jnp.concatenate([h] + halos, axis=0)
    f = wd_ref.shape[0]
    row = lax.broadcasted_iota(jnp.int32, (TILE, 1), 0)
    for c in range(f // FFN_COLS):
        cs = slice(c * FFN_COLS, (c + 1) * FFN_COLS)
        ge = _dot(h_ext, wu_ref[:, cs])
        val = _dot(h, wu_ref[:, f + c * FFN_COLS:f + (c + 1) * FFN_COLS])
        for r in range(nb):
            g = ge[r * TILE:(r + 1) * TILE]
            hb = nb * TILE + 2 * SUBLANES * r
            before = ge[hb + SUBLANES - 1:hb + SUBLANES]
            after = ge[hb + SUBLANES:hb + SUBLANES + 1]
            up = jnp.where(row == 0, before, pltpu.roll(g, 1, axis=0))
            dn = jnp.where(row == TILE - 1, after, pltpu.roll(g, TILE - 1, axis=0))
            conv = up * cw_ref[0:1, cs] + g * cw_ref[1:2, cs] + dn * cw_ref[2:3, cs] + cb_ref[:, cs]
            u_ref[r * TILE:(r + 1) * TILE, cs] = (_silu(conv) * val[r * TILE:(r + 1) * TILE]).astype(BF16)
    acc = _dot(u_ref[...], wd_ref[...])
    for r in range(nb):
        y_ref[r] = xs[r] + mods[r][:, 5 * d:6 * d] * acc[r * TILE:(r + 1) * TILE]


def _ffn(st, xs, mod, gain, w_up, conv_w, conv_b, w_down):
    d = st.d
    f = w_down.shape[0]
    per_tile = TILE // SUBLANES
    last = st.rows // SUBLANES - 1
    return pl.pallas_call(
        functools.partial(_ffn_kernel, st=st),
        grid=st.grid,
        in_specs=[st.tile(d),
                  pl.BlockSpec((st.nb, SUBLANES, d), lambda b, t: (b, jnp.maximum(t * per_tile - 1, 0), 0)),
                  pl.BlockSpec((st.nb, SUBLANES, d), lambda b, t: (b, jnp.minimum((t + 1) * per_tile, last), 0)),
                  *st.mod(), st.row(d), _resident(w_up.shape), _resident(conv_w.shape), _resident(conv_b.shape),
                  _resident(w_down.shape)],
        out_specs=st.tile(d),
        out_shape=jax.ShapeDtypeStruct((st.batch, st.rows, d), F32),
        scratch_shapes=[pltpu.VMEM((st.nb * TILE, f), BF16)],
        compiler_params=_params(("parallel", "parallel")),
        name="ffn",
    )(xs, xs, xs, mod, mod, gain, w_up, conv_w, conv_b, w_down)


def _rope_tables(seq, ctx):
    rows = seq // GRID_W
    row = jnp.repeat(jnp.arange(rows, dtype=F32), GRID_W)
    col = jnp.tile(jnp.arange(GRID_W, dtype=F32), rows)
    n_pairs = HEAD_DIM // 4
    inv = ROPE_THETA ** (-jnp.arange(n_pairs, dtype=F32) / n_pairs)
    ang_r, ang_c = row[:, None] * inv, col[:, None] * inv
    cos = jnp.concatenate([jnp.cos(ang_r)] * 2 + [jnp.cos(ang_c)] * 2, axis=1)
    sin = jnp.concatenate([-jnp.sin(ang_r), jnp.sin(ang_r), -jnp.sin(ang_c), jnp.sin(ang_c)], axis=1)
    cos = jnp.concatenate([cos, jnp.ones((ctx, HEAD_DIM), F32)], axis=0)
    sin = jnp.concatenate([sin, jnp.zeros((ctx, HEAD_DIM), F32)], axis=0)
    return jnp.tile(cos, (1, 4)), jnp.tile(sin, (1, 4))


def _head_block_constants():
    blk = 4 * HEAD_DIM
    i = np.arange(blk)
    gm = (i[:, None] // HEAD_DIM == i[None, :] // HEAD_DIM).astype(np.float32) / HEAD_DIM
    quarter = HEAD_DIM // 4
    partner = np.where((i % (2 * quarter)) < quarter, i + quarter, i - quarter)
    rot = (i[:, None] == partner[None, :]).astype(np.float32)
    return jnp.asarray(gm, BF16), jnp.asarray(rot, BF16)


def _query_head_order():
    return np.array([kh * GROUP + g for g in range(GROUP) for kh in range(N_KV_HEADS)])


def kernel(x, c, ctx, c_ctx, ada_w, ada_b, norm1_g, norm2_g, attn_w_in, attn_w_out, attn_q_gain, attn_k_gain,
           attn_sink, hgrn_w_in, hgrn_w_out, hgrn_o_gain, hgrn_lb_logits, ffn_w_up, ffn_conv_w, ffn_conv_b,
           ffn_w_down):
    batch, seq, d = x.shape
    n_ctx_tok = ctx.shape[1]
    depth = ada_w.shape[0]
    assert seq % TILE == 0 and n_ctx_tok % TILE == 0 and seq >= QBLK + 2 * WINDOW
    n_lat, n_con = seq // TILE, n_ctx_tok // TILE
    st2 = _Stream(batch, n_lat, n_con, d, STREAM_ROWS)
    st4 = _Stream(batch, n_lat, n_con, d, WIDE_ROWS)
    st4_lat = _Stream(batch, n_lat, 0, d, WIDE_ROWS)

    rows = -(-(batch + 1) // SUBLANES) * SUBLANES
    cc = jnp.zeros((rows, d), F32).at[:batch].set(c).at[batch].set(c_ctx)
    mod_all = _modulation(cc, ada_w, ada_b)[:, :batch + 1].reshape(depth, batch + 1, 1, 6 * d)

    cos, sin = _rope_tables(seq, n_ctx_tok)
    gm, rot = _head_block_constants()
    order = _query_head_order()
    nq = N_HEADS * HEAD_DIM
    lvl_f, lvl_b = jnp.asarray(_level_table(False), BF16), jnp.asarray(_level_table(True), BF16)

    lat, con = x, ctx
    for layer in range(depth):
        j = layer // 2
        mod = mod_all[layer]
        g1 = norm1_g[layer].reshape(1, d)
        st_out = st4_lat if layer == depth - 1 else st4
        if layer % 2 == 0:
            w_in = attn_w_in[j]
            wq = w_in[:, :nq].reshape(d, N_HEADS, HEAD_DIM)[:, order].reshape(d, nq)
            w_in = jnp.concatenate([wq, w_in[:, nq:]], axis=1).astype(BF16)
            w_out = attn_w_out[j].reshape(N_HEADS, HEAD_DIM, d)[order].reshape(nq, d).astype(BF16)
            qg = jnp.tile(attn_q_gain[j], 4).reshape(1, 4 * HEAD_DIM)
            kg = jnp.tile(attn_k_gain[j], 4).reshape(1, 4 * HEAD_DIM)
            q, k, v = _attn_proj(st4, lat, con, mod, g1, w_in, qg, kg, cos, sin, gm, rot)
            xs = _attention(attn_sink[j][order], lat, con, mod, q, k, v, w_out, seq, n_ctx_tok)
        else:
            q, kf, bf, kb, bb, v, gate = _hgrn_proj(st2, lat, mod, g1, hgrn_w_in[j].astype(BF16), hgrn_lb_logits,
                                                    layer)
            o_f = _hgrn_scan_fwd(lvl_f, q, kf, bf, v, seq, n_ctx_tok)
            xs = _hgrn_scan_bwd_out(lvl_b, q, kb, bb, v, o_f, gate, lat, mod, hgrn_o_gain[j].reshape(1, HG_DK),
                                    hgrn_w_out[j].astype(BF16), seq, n_ctx_tok)
        xs = _ffn(st_out, xs, mod, norm2_g[layer].reshape(1, d), ffn_w_up[layer].astype(BF16), ffn_conv_w[layer],
                  ffn_conv_b[layer].reshape(1, -1), ffn_w_down[layer].astype(BF16))
        lat = con = xs
    return xs
```
